```python
import math
import jax, jax.numpy as jnp
from jax import lax
import numpy as np

D_MODEL = 1024
BATCH = 32
SEQ = 2048
DEPTH = 1
DEC_BATCH = 16
DEC_SEQ = 32
PAST_LEN = 2048

CHUNK = 64
GDN_HEADS = 8
GDN_DK = 128
GDN_DV = 128
GDN_DIM = GDN_HEADS * GDN_DK
GDN_VDIM = GDN_HEADS * GDN_DV
QKV_DIM = 2 * GDN_DIM + GDN_VDIM
CONV_W = 4
LRU_WIDTH = D_MODEL
LRU_BLOCKS = 8
LRU_BW = LRU_WIDTH // LRU_BLOCKS
LRU_C = 8.0
V_END = QKV_DIM
Z_END = V_END + GDN_VDIM
B_END = Z_END + GDN_HEADS
A_END = B_END + GDN_HEADS
LX_END = A_END + LRU_WIDTH
IN_DIM = LX_END + LRU_WIDTH
MIX_DIM = GDN_VDIM + LRU_WIDTH
N_EXPERTS = 64
TOP_K = 8
N_GROUPS = 8
TOPK_GROUPS = 4
EXPERT_FF = 256
SHARED_FF = 256
ROUTED_SCALE = 2.5
MOE_BLOCK = 128
DN_ALPHA = (2.0 * DEPTH) ** 0.25
DN_BETA = (8.0 * DEPTH) ** -0.25
LN_EPS = 1e-5
RMS_EPS = 1e-6

kernel_name = 'hybrid_gdn_rglru_moe_stream_step'


def layer_norm(x, g, b):
    xf = x.astype(jnp.float32)
    mu = jnp.mean(xf, -1, keepdims=True)
    xc = xf - mu
    var = jnp.mean(xc * xc, -1, keepdims=True)
    return (xc * lax.rsqrt(var + LN_EPS) * g + b).astype(x.dtype)


def l2norm(t):
    return t * lax.rsqrt(jnp.sum(t * t, -1, keepdims=True) + 1e-6)


def causal_dwconv(x, buf, w, b=None):
    T = x.shape[1]
    xp = jnp.concatenate([buf.astype(x.dtype), x], axis=1)
    out = xp[:, 0:T] * w[0]
    for j in range(1, CONV_W):
        out = out + xp[:, j:j + T] * w[j]
    if b is not None:
        out = out + b
    return out, xp[:, T:]


def gdn_chunked(q, k, v, beta, g, s0, chunk):
    bsz, T, H, dk = q.shape
    dv = v.shape[-1]
    n = T // chunk

    def blocks(t):
        t = t.reshape((bsz, n, chunk, H) + t.shape[3:])
        return jnp.moveaxis(t, 3, 1)

    q, k, v, beta, g = blocks(q), blocks(k), blocks(v), blocks(beta), blocks(g)
    g = jnp.cumsum(g, axis=-1)
    causal = jnp.tril(jnp.ones((chunk, chunk), bool))
    strict = jnp.tril(jnp.ones((chunk, chunk), bool), -1)
    decay = jnp.exp(jnp.where(causal, g[..., :, None] - g[..., None, :], -jnp.inf))
    k_beta = k * beta[..., None]
    lower = jnp.where(strict, jnp.einsum('bhnid,bhnjd->bhnij', k_beta, k) * decay, 0.0)
    m = lower + jnp.eye(chunk, dtype=lower.dtype)
    rhs = jnp.concatenate([v * beta[..., None], k_beta * jnp.exp(g)[..., None]], -1)
    sol = lax.linalg.triangular_solve(m, rhs, left_side=True, lower=True, unit_diagonal=True)
    u, w = sol[..., :dv], sol[..., dv:]
    qk = jnp.einsum('bhnid,bhnjd->bhnij', q, k) * decay
    g_last = g[..., -1]
    q_dec = q * jnp.exp(g)[..., None]
    k_dec = k * jnp.exp(g_last[..., None] - g)[..., None]

    def step(s, inp):
        u_c, w_c, qk_c, qd_c, kd_c, gl_c = inp
        v_new = u_c - jnp.einsum('bhcd,bhde->bhce', w_c, s)
        o_c = jnp.einsum('bhcd,bhde->bhce', qd_c, s) + jnp.einsum('bhij,bhje->bhie', qk_c, v_new)
        s = s * jnp.exp(gl_c)[..., None, None] + jnp.einsum('bhcd,bhce->bhde', kd_c, v_new)
        return s, o_c

    xs = tuple(jnp.moveaxis(t, 2, 0) for t in (u, w, qk, q_dec, k_dec, g_last))
    s_fin, o = lax.scan(step, s0, xs)
    o = jnp.moveaxis(jnp.moveaxis(o, 0, 2), 1, 3).reshape(bsz, T, H, dv)
    return o, s_fin


def _lin_combine(e1, e2):
    a1, b1 = e1
    a2, b2 = e2
    return a1 * a2, a2 * b1 + b2


def route(x2, router_w, router_bias):
    T = x2.shape[0]
    scores = jax.nn.sigmoid((x2 @ router_w).astype(jnp.float32))
    biased = scores + router_bias.astype(jnp.float32)
    grp = biased.reshape(T, N_GROUPS, N_EXPERTS // N_GROUPS)
    grp_score = lax.top_k(grp, 2)[0].sum(-1)
    _, gidx = lax.top_k(grp_score, TOPK_GROUPS)
    gmask = jax.nn.one_hot(gidx, N_GROUPS).sum(-2) > 0
    emask = jnp.repeat(gmask, N_EXPERTS // N_GROUPS, axis=-1)
    _, idx = lax.top_k(jnp.where(emask, biased, -jnp.inf), TOP_K)
    gw = jnp.take_along_axis(scores, idx, -1)
    gw = gw / jnp.sum(gw, -1, keepdims=True) * ROUTED_SCALE
    return idx, gw


def routed_experts(x2, idx, gw, w_gate, w_up, w_down):
    T, D = x2.shape
    A = T * TOP_K
    n_blocks = (A + N_EXPERTS * (MOE_BLOCK - 1)) // MOE_BLOCK + 1
    flat_e = idx.reshape(A)
    order = jnp.argsort(flat_e)
    e_sorted = flat_e[order]
    counts = jnp.bincount(flat_e, length=N_EXPERTS)
    padded = (counts + MOE_BLOCK - 1) // MOE_BLOCK * MOE_BLOCK
    pad_end = jnp.cumsum(padded)
    pad_start = pad_end - padded
    start = jnp.cumsum(counts) - counts
    dest = pad_start[e_sorted] + jnp.arange(A) - start[e_sorted]
    row_tok = jnp.full((n_blocks * MOE_BLOCK,), T, jnp.int32).at[dest].set((order // TOP_K).astype(jnp.int32))
    row_gate = jnp.zeros((n_blocks * MOE_BLOCK,), jnp.float32).at[dest].set(gw.reshape(A)[order])
    blk_e = jnp.minimum(jnp.searchsorted(pad_end, jnp.arange(n_blocks) * MOE_BLOCK, side='right'), N_EXPERTS - 1)
    x_pad = jnp.concatenate([x2, jnp.zeros((1, D), x2.dtype)], 0)

    def one_block(args):
        rows, gates, e = args
        xb = x_pad[rows]
        h = jax.nn.silu(xb @ w_gate[e]) * (xb @ w_up[e])
        return (h @ w_down[e]) * gates[:, None].astype(xb.dtype)

    ys = lax.map(one_block, (row_tok.reshape(n_blocks, MOE_BLOCK), row_gate.reshape(n_blocks, MOE_BLOCK), blk_e))
    return jax.ops.segment_sum(ys.reshape(-1, D), row_tok, num_segments=T + 1)[:T]


def hybrid_layer(x, gdn_buf, s0, lru_buf, h0, start_pos, gdn_chunk,
                 w_in, gdn_conv_w, gdn_a_log, gdn_dt_bias, gdn_norm_w,
                 lru_conv_w, lru_conv_b, lru_wa, lru_ba, lru_wx, lru_bx, lru_lambda,
                 w_out, ln1_g, ln1_b, router_w, router_bias,
                 exp_w_gate, exp_w_up, exp_w_down, sh_w_gate, sh_w_up, sh_w_down, ln2_g, ln2_b):
    f32 = jnp.float32
    bsz, T, _ = x.shape
    proj = x @ w_in
    qkv_raw, z, b_raw, a_raw, lx, ly = jnp.split(proj, [V_END, Z_END, B_END, A_END, LX_END], axis=-1)

    qkv, gdn_buf_new = causal_dwconv(qkv_raw, gdn_buf, gdn_conv_w)
    qkv = jax.nn.silu(qkv.astype(f32))
    q, k, v = jnp.split(qkv, [GDN_DIM, 2 * GDN_DIM], axis=-1)
    q = l2norm(q.reshape(bsz, T, GDN_HEADS, GDN_DK)) * (GDN_DK ** -0.5)
    k = l2norm(k.reshape(bsz, T, GDN_HEADS, GDN_DK))
    v = v.reshape(bsz, T, GDN_HEADS, GDN_DV)
    beta = jax.nn.sigmoid(b_raw.astype(f32))
    g = -jnp.exp(gdn_a_log.astype(f32)) * jax.nn.softplus(a_raw.astype(f32) + gdn_dt_bias.astype(f32))
    o, s_new = gdn_chunked(q, k, v, beta, g, s0.astype(f32), gdn_chunk)
    o = o * lax.rsqrt(jnp.mean(o * o, -1, keepdims=True) + RMS_EPS) * gdn_norm_w.astype(f32)
    o = o * jax.nn.silu(z.astype(f32).reshape(bsz, T, GDN_HEADS, GDN_DV))
    o = o.reshape(bsz, T, GDN_VDIM)

    xc, lru_buf_new = causal_dwconv(lx, lru_buf, lru_conv_w, lru_conv_b)
    xc = xc.astype(f32)
    xb = xc.reshape(bsz, T, LRU_BLOCKS, LRU_BW)
    r = jax.nn.sigmoid(jnp.einsum('btki,kij->btkj', xb, lru_wa.astype(f32)).reshape(bsz, T, LRU_WIDTH) + lru_ba.astype(f32))
    i = jax.nn.sigmoid(jnp.einsum('btki,kij->btkj', xb, lru_wx.astype(f32)).reshape(bsz, T, LRU_WIDTH) + lru_bx.astype(f32))
    log_a = LRU_C * r * jax.nn.log_sigmoid(lru_lambda.astype(f32))
    a = jnp.exp(log_a)
    mult = jnp.sqrt(-jnp.expm1(2.0 * log_a))
    pos = start_pos + jnp.arange(T)
    mult = jnp.where((pos == 0)[None, :, None], 1.0, mult)
    bt = mult * i * xc
    bt = bt.at[:, 0].add(a[:, 0] * h0.astype(f32))
    _, h = lax.associative_scan(_lin_combine, (a, bt), axis=1)
    h_new = h[:, -1]
    lru_out = h * jax.nn.gelu(ly.astype(f32))

    mix = jnp.concatenate([o, lru_out], -1).astype(x.dtype) @ w_out
    h1 = layer_norm(DN_ALPHA * x + mix, ln1_g, ln1_b)

    x2 = h1.reshape(-1, D_MODEL)
    idx, gw = route(x2, router_w, router_bias)
    shared = (jax.nn.silu(x2 @ sh_w_gate) * (x2 @ sh_w_up)) @ sh_w_down
    ffn = shared + routed_experts(x2, idx, gw, exp_w_gate, exp_w_up, exp_w_down).astype(shared.dtype)
    y = layer_norm(DN_ALPHA * h1 + ffn.reshape(h1.shape), ln2_g, ln2_b)
    dt = x.dtype
    return y, gdn_buf_new.astype(dt), s_new.astype(dt), lru_buf_new.astype(dt), h_new.astype(dt)


def setup_inputs(seed: int = 0) -> dict:
    key = jax.random.key(seed)
    ks = jax.random.split(key, 40)
    f32 = jnp.float32
    L = DEPTH

    def nrm(k, shape, scale):
        return jax.random.normal(k, shape, f32) * scale

    dt = jnp.exp(jax.random.uniform(ks[10], (L, GDN_HEADS), f32, math.log(1e-3), math.log(1e-1)))
    s_lam = jax.random.uniform(ks[15], (L, LRU_WIDTH), f32, 0.9, 0.999) ** (1.0 / LRU_C)
    return {
        'x_prompt': nrm(ks[0], (BATCH, SEQ, D_MODEL), 1.0),
        'x_sample': nrm(ks[1], (DEC_BATCH, DEC_SEQ, D_MODEL), 1.0),
        'state_gdn_conv': nrm(ks[2], (L, DEC_BATCH, CONV_W - 1, QKV_DIM), 1.0),
        'state_gdn': nrm(ks[3], (L, DEC_BATCH, GDN_HEADS, GDN_DK, GDN_DV), 0.5),
        'state_lru_conv': nrm(ks[4], (L, DEC_BATCH, CONV_W - 1, LRU_WIDTH), 1.0),
        'state_lru': nrm(ks[5], (L, DEC_BATCH, LRU_WIDTH), 1.0),
        'w_in': nrm(ks[6], (L, D_MODEL, IN_DIM), D_MODEL ** -0.5),
        'gdn_conv_w': nrm(ks[7], (L, CONV_W, QKV_DIM), CONV_W ** -0.5),
        'gdn_a_log': jnp.log(jax.random.uniform(ks[8], (L, GDN_HEADS), f32, 1.0, 16.0)),
        'gdn_dt_bias': dt + jnp.log(-jnp.expm1(-dt)),
        'gdn_norm_w': 1.0 + nrm(ks[9], (L, GDN_DV), 0.02),
        'lru_conv_w': nrm(ks[11], (L, CONV_W, LRU_WIDTH), CONV_W ** -0.5),
        'lru_conv_b': nrm(ks[12], (L, LRU_WIDTH), 0.02),
        'lru_wa': nrm(ks[13], (L, LRU_BLOCKS, LRU_BW, LRU_BW), LRU_BW ** -0.5),
        'lru_ba': nrm(ks[14], (L, LRU_WIDTH), 0.02),
        'lru_wx': nrm(ks[16], (L, LRU_BLOCKS, LRU_BW, LRU_BW), LRU_BW ** -0.5),
        'lru_bx': nrm(ks[17], (L, LRU_WIDTH), 0.02),
        'lru_lambda': jnp.log(s_lam) - jnp.log1p(-s_lam),
        'w_out': nrm(ks[18], (L, MIX_DIM, D_MODEL), MIX_DIM ** -0.5 * DN_BETA),
        'ln1_g': 1.0 + nrm(ks[19], (L, D_MODEL), 0.02),
        'ln1_b': nrm(ks[20], (L, D_MODEL), 0.02),
        'router_w': nrm(ks[21], (L, D_MODEL, N_EXPERTS), D_MODEL ** -0.5),
        'router_bias': nrm(ks[22], (L, N_EXPERTS), 0.01),
        'exp_w_gate': nrm(ks[23], (L, N_EXPERTS, D_MODEL, EXPERT_FF), D_MODEL ** -0.5),
        'exp_w_up': nrm(ks[24], (L, N_EXPERTS, D_MODEL, EXPERT_FF), D_MODEL ** -0.5),
        'exp_w_down': nrm(ks[25], (L, N_EXPERTS, EXPERT_FF, D_MODEL), EXPERT_FF ** -0.5 * DN_BETA),
        'sh_w_gate': nrm(ks[26], (L, D_MODEL, SHARED_FF), D_MODEL ** -0.5),
        'sh_w_up': nrm(ks[27], (L, D_MODEL, SHARED_FF), D_MODEL ** -0.5),
        'sh_w_down': nrm(ks[28], (L, SHARED_FF, D_MODEL), SHARED_FF ** -0.5 * DN_BETA),
        'ln2_g': 1.0 + nrm(ks[29], (L, D_MODEL), 0.02),
        'ln2_b': nrm(ks[30], (L, D_MODEL), 0.02),
    }


def reference(x_prompt, x_sample, state_gdn_conv, state_gdn, state_lru_conv, state_lru,
              w_in, gdn_conv_w, gdn_a_log, gdn_dt_bias, gdn_norm_w,
              lru_conv_w, lru_conv_b, lru_wa, lru_ba, lru_wx, lru_bx, lru_lambda,
              w_out, ln1_g, ln1_b, router_w, router_bias,
              exp_w_gate, exp_w_up, exp_w_down, sh_w_gate, sh_w_up, sh_w_down, ln2_g, ln2_b):
    yp, ys = x_prompt, x_sample
    bp = x_prompt.shape[0]
    gc_p, gs_p, lc_p, ls_p = [], [], [], []
    gc_s, gs_s, lc_s, ls_s = [], [], [], []
    for l in range(DEPTH):
        lw = (w_in[l], gdn_conv_w[l], gdn_a_log[l], gdn_dt_bias[l], gdn_norm_w[l],
              lru_conv_w[l], lru_conv_b[l], lru_wa[l], lru_ba[l], lru_wx[l], lru_bx[l], lru_lambda[l],
              w_out[l], ln1_g[l], ln1_b[l], router_w[l], router_bias[l],
              exp_w_gate[l], exp_w_up[l], exp_w_down[l], sh_w_gate[l], sh_w_up[l], sh_w_down[l],
              ln2_g[l], ln2_b[l])
        yp, a1, a2, a3, a4 = hybrid_layer(
            yp,
            jnp.zeros((bp, CONV_W - 1, QKV_DIM), yp.dtype),
            jnp.zeros((bp, GDN_HEADS, GDN_DK, GDN_DV), jnp.float32),
            jnp.zeros((bp, CONV_W - 1, LRU_WIDTH), yp.dtype),
            jnp.zeros((bp, LRU_WIDTH), jnp.float32),
            0, CHUNK, *lw)
        gc_p.append(a1); gs_p.append(a2); lc_p.append(a3); ls_p.append(a4)
        ys, b1, b2, b3, b4 = hybrid_layer(
            ys, state_gdn_conv[l], state_gdn[l], state_lru_conv[l], state_lru[l],
            PAST_LEN, ys.shape[1], *lw)
        gc_s.append(b1); gs_s.append(b2); lc_s.append(b3); ls_s.append(b4)
    return (yp, ys,
            jnp.stack(gc_p, 0), jnp.stack(gs_p, 0), jnp.stack(lc_p, 0), jnp.stack(ls_p, 0),
            jnp.stack(gc_s, 0), jnp.stack(gs_s, 0), jnp.stack(lc_s, 0), jnp.stack(ls_s, 0))
```

```python
import functools
import math

import jax
import jax.numpy as jnp
from jax import lax
from jax.experimental import pallas as pl
from jax.experimental.pallas import tpu as pltpu

F32 = jnp.float32
I32 = jnp.int32
MXU_DTYPE = jnp.bfloat16

D_MODEL = 1024
HEADS = 8
HEAD_DIM = 128
QKV_GROUPS = 3 * HEADS
LRU_BLOCKS = 8
CONV_W = 4
CONV_PAD = 8
N_EXPERTS = 64
N_GROUPS = 8
GROUP_SIZE = N_EXPERTS // N_GROUPS
TOPK_GROUPS = 4
TOP_K = 8
EXPERT_FF = 256
SHARED_FF = 256
ROUTED_SCALE = 2.5
LRU_C = 8.0
DEPTH = 1
DN_ALPHA = (2.0 * DEPTH) ** 0.25
LN_EPS = 1e-5
RMS_EPS = 1e-6
L2_EPS = 1e-6

LANES = 128
SUBLANES = 8
BF16_ROWS = 16
VMEM_LIMIT_BYTES = 56 * 1024 * 1024

MIX_TILE = 256
TOK_TILE = 256
ROW_BLOCK = 256


def _dot(a, b):
    return jnp.dot(a.astype(MXU_DTYPE), b.astype(MXU_DTYPE), preferred_element_type=F32)


def _dot_nt(a, b):
    return lax.dot_general(a.astype(MXU_DTYPE), b.astype(MXU_DTYPE),
                           (((1,), (1,)), ((), ())), preferred_element_type=F32)


def _dot_tn(a, b):
    return lax.dot_general(a.astype(MXU_DTYPE), b.astype(MXU_DTYPE),
                           (((0,), (0,)), ((), ())), preferred_element_type=F32)


def _dot_f32(a, b):
    return jnp.dot(a, b, precision=lax.Precision.HIGHEST, preferred_element_type=F32)


def _sigmoid(x):
    return 1.0 / (1.0 + jnp.exp(-x))


def _softplus(x):
    return jnp.maximum(x, 0.0) + jnp.log1p(jnp.exp(-jnp.abs(x)))


def _widen(col, width):
    if width <= LANES:
        return col[:, :width]
    return jnp.concatenate([col] * (width // LANES), axis=1)


def _mixer_kernel(x_ref, gbuf0_ref, s0_ref, lbuf0_ref, h0_ref,
                  wmain_ref, wba_ref, gcw_ref, alog_ref, dtb_ref, gnw_ref,
                  lcw_ref, lcb_ref, wax_ref, lbax_ref, lam_ref,
                  mix_ref, gbuf_ref, s_ref, lbuf_ref, h_ref,
                  qkv_s, z_s, lx_s, ly_s, col_s, row_s, o_s, hst_s,
                  *, tile, chunk, reset_first):
    TT, C = tile, chunk
    NC = TT // C
    t = pl.program_id(1)

    @pl.when(t == 0)
    def _load_state():
        for g in range(QKV_GROUPS):
            qkv_s[g, 0:CONV_PAD, :] = gbuf0_ref[0, :, g * LANES:(g + 1) * LANES]
        for k in range(LRU_BLOCKS):
            lx_s[k, 0:CONV_PAD, :] = lbuf0_ref[0, :, k * LANES:(k + 1) * LANES]
        s_ref[...] = s0_ref[...]
        hst_s[...] = h0_ref[0]

    xb = x_ref[0].astype(MXU_DTYPE)
    for gp in range(QKV_GROUPS // 2):
        res = jnp.dot(xb, wmain_ref[:, gp * 256:(gp + 1) * 256], preferred_element_type=F32)
        qkv_s[2 * gp, CONV_PAD:CONV_PAD + TT, :] = res[:, :LANES]
        qkv_s[2 * gp + 1, CONV_PAD:CONV_PAD + TT, :] = res[:, LANES:]
    col0 = QKV_GROUPS * LANES
    for gp in range(HEADS // 2):
        res = jnp.dot(xb, wmain_ref[:, col0 + gp * 256:col0 + (gp + 1) * 256], preferred_element_type=F32)
        z_s[2 * gp] = res[:, :LANES]
        z_s[2 * gp + 1] = res[:, LANES:]
    col0 += HEADS * LANES
    for gp in range(LRU_BLOCKS // 2):
        res = jnp.dot(xb, wmain_ref[:, col0 + gp * 256:col0 + (gp + 1) * 256], preferred_element_type=F32)
        lx_s[2 * gp, CONV_PAD:CONV_PAD + TT, :] = res[:, :LANES]
        lx_s[2 * gp + 1, CONV_PAD:CONV_PAD + TT, :] = res[:, LANES:]
    col0 += LRU_BLOCKS * LANES
    for gp in range(LRU_BLOCKS // 2):
        res = jnp.dot(xb, wmain_ref[:, col0 + gp * 256:col0 + (gp + 1) * 256], preferred_element_type=F32)
        ly_s[2 * gp] = res[:, :LANES]
        ly_s[2 * gp + 1] = res[:, LANES:]
    ba = jnp.dot(xb, wba_ref[...], preferred_element_type=F32)

    for g in range(QKV_GROUPS):
        gbuf_ref[0, :, g * LANES:(g + 1) * LANES] = qkv_s[g, TT:TT + CONV_PAD, :]
    for k in range(LRU_BLOCKS):
        lbuf_ref[0, :, k * LANES:(k + 1) * LANES] = lx_s[k, TT:TT + CONV_PAD, :]

    row = lax.broadcasted_iota(I32, (TT, TT), 0)
    colm = lax.broadcasted_iota(I32, (TT, TT), 1)
    same_chunk = (row // C) == (colm // C)
    causal = same_chunk & (row >= colm)
    strict = same_chunk & (row > colm)
    beta_all = _sigmoid(ba)
    g_all = -jnp.exp(alog_ref[...]) * _softplus(ba + dtb_ref[...])
    gc_all = _dot_f32(causal.astype(F32), g_all)
    gt_all = _dot_f32(same_chunk.astype(F32), g_all)
    gc_t = gc_all.T
    for h in range(HEADS):
        col_s[0, h] = jnp.broadcast_to(beta_all[:, h:h + 1], (TT, LANES))
        col_s[1, h] = jnp.broadcast_to(gc_all[:, HEADS + h:HEADS + h + 1], (TT, LANES))
        col_s[2, h] = jnp.broadcast_to(gt_all[:, HEADS + h:HEADS + h + 1], (TT, LANES))
        row_s[h] = gc_t[HEADS + h:HEADS + h + 1, :]

    def conv_silu(g):
        w = gcw_ref[g]
        acc = qkv_s[g, pl.ds(CONV_PAD - 3, TT), :] * w[0:1]
        acc = acc + qkv_s[g, pl.ds(CONV_PAD - 2, TT), :] * w[1:2]
        acc = acc + qkv_s[g, pl.ds(CONV_PAD - 1, TT), :] * w[2:3]
        acc = acc + qkv_s[g, pl.ds(CONV_PAD, TT), :] * w[3:4]
        return acc * _sigmoid(acc)

    def head_body(h, carry):
        q = conv_silu(h)
        k = conv_silu(HEADS + h)
        v = conv_silu(2 * HEADS + h)
        q = q * lax.rsqrt(jnp.sum(q * q, axis=-1, keepdims=True) + L2_EPS) * (HEAD_DIM ** -0.5)
        k = k * lax.rsqrt(jnp.sum(k * k, axis=-1, keepdims=True) + L2_EPS)
        beta = col_s[0, h]
        gcol = col_s[1, h]
        gtot = col_s[2, h]
        grow = row_s[h]
        decay = jnp.where(causal, jnp.exp(_widen(gcol, TT) - grow), 0.0)
        eg = jnp.exp(gcol)
        kb = k * beta
        qks = _dot_nt(jnp.concatenate([kb, q], axis=0), k)
        nmat = jnp.where(strict, -(qks[:TT] * decay), 0.0)
        qk = qks[TT:] * decay
        tm = nmat
        n_levels = int(math.log2(C))
        if n_levels > 1:
            npow = _dot(nmat, nmat)
            for j in range(1, n_levels):
                if j < n_levels - 1:
                    r2 = _dot(jnp.concatenate([tm, npow], axis=0), npow)
                    tm = tm + npow + r2[:TT]
                    npow = r2[TT:]
                else:
                    tm = tm + npow + _dot(tm, npow)
        rhs = jnp.concatenate([v * beta, kb * eg], axis=1)
        uw = rhs + _dot(tm, rhs)
        u = uw[:, :LANES]
        w = uw[:, LANES:]
        qd = q * eg
        kd = k * jnp.exp(gtot - gcol)
        egt = jnp.exp(gtot)
        state = s_ref[0, h]
        vn_parts = []
        oi_parts = []
        for c in range(NC):
            sl = slice(c * C, (c + 1) * C)
            wq = _dot(jnp.concatenate([w[sl], qd[sl]], axis=0), state)
            vn = u[sl] - wq[:C]
            vn_parts.append(vn)
            oi_parts.append(wq[C:])
            state = state * egt[c * C:c * C + 1, :] + _dot_tn(kd[sl], vn)
        s_ref[0, h] = state
        vn_all = vn_parts[0] if NC == 1 else jnp.concatenate(vn_parts, axis=0)
        o = (oi_parts[0] if NC == 1 else jnp.concatenate(oi_parts, axis=0)) + _dot(qk, vn_all)
        o = o * lax.rsqrt(jnp.mean(o * o, axis=-1, keepdims=True) + RMS_EPS) * gnw_ref[...]
        zz = z_s[h]
        o_s[h] = o * (zz * _sigmoid(zz))
        return carry

    lax.fori_loop(0, HEADS, head_body, 0)

    G = TT // SUBLANES
    sub = lax.broadcasted_iota(I32, (G, SUBLANES, LANES), 1)
    rowi = lax.broadcasted_iota(I32, (TT, LANES), 0)
    first_row = rowi == 0

    def lru_body(kb_, carry):
        w = lcw_ref[kb_]
        xc = lx_s[kb_, pl.ds(CONV_PAD - 3, TT), :] * w[0:1]
        xc = xc + lx_s[kb_, pl.ds(CONV_PAD - 2, TT), :] * w[1:2]
        xc = xc + lx_s[kb_, pl.ds(CONV_PAD - 1, TT), :] * w[2:3]
        xc = xc + lx_s[kb_, pl.ds(CONV_PAD, TT), :] * w[3:4]
        xc = xc + lcb_ref[kb_]
        gates = _sigmoid(_dot(xc, wax_ref[kb_]) + lbax_ref[kb_])
        r = gates[:, :LANES]
        gi = gates[:, LANES:]
        log_a = (LRU_C * r) * (-_softplus(-lam_ref[kb_]))
        a = jnp.exp(log_a)
        th = jnp.tanh(log_a)
        mult = jnp.sqrt(-2.0 * th / (1.0 - th))
        if reset_first:
            mult = jnp.where(first_row & (t == 0), 1.0, mult)
        hprev = hst_s[kb_]
        b = mult * gi * xc + jnp.where(first_row, a * hprev, 0.0)
        a3 = a.reshape(G, SUBLANES, LANES)
        b3 = b.reshape(G, SUBLANES, LANES)
        for s in (1, 2, 4):
            a_sh = jnp.where(sub >= s, pltpu.roll(a3, s, 1), 1.0)
            b_sh = jnp.where(sub >= s, pltpu.roll(b3, s, 1), 0.0)
            b3 = a3 * b_sh + b3
            a3 = a3 * a_sh
        hs = []
        hc = jnp.zeros((1, LANES), F32)
        for gidx in range(G):
            hg = b3[gidx] if gidx == 0 else a3[gidx] * hc + b3[gidx]
            hc = hg[SUBLANES - 1:SUBLANES, :]
            hs.append(hg)
        hall = jnp.concatenate(hs, axis=0)
        hst_s[kb_] = hc
        o_s[HEADS + kb_] = hall * jax.nn.gelu(ly_s[kb_])
        return carry

    lax.fori_loop(0, LRU_BLOCKS, lru_body, 0)

    for g in range(HEADS + LRU_BLOCKS):
        mix_ref[0, :, g * LANES:(g + 1) * LANES] = o_s[g].astype(mix_ref.dtype)
    h_ref[0] = hst_s[...]
    for g in range(QKV_GROUPS):
        qkv_s[g, 0:CONV_PAD, :] = qkv_s[g, TT:TT + CONV_PAD, :]
    for k in range(LRU_BLOCKS):
        lx_s[k, 0:CONV_PAD, :] = lx_s[k, TT:TT + CONV_PAD, :]


def _const_spec(shape):
    nd = len(shape)
    return pl.BlockSpec(shape, lambda b, t, _n=nd: (0,) * _n)


def _mixer(x, gbuf0, s0, lbuf0, h0, wts, *, chunk, reset_first, interpret=False):
    B, T, _ = x.shape
    TT = min(MIX_TILE, T)
    assert T % TT == 0 and TT % chunk == 0 and TT % SUBLANES == 0
    NT = T // TT
    kern = functools.partial(_mixer_kernel, tile=TT, chunk=chunk, reset_first=reset_first)
    in_specs = [
        pl.BlockSpec((1, TT, D_MODEL), lambda b, t: (b, t, 0)),
        pl.BlockSpec((1, CONV_PAD, QKV_GROUPS * LANES), lambda b, t: (b, 0, 0)),
        pl.BlockSpec((1, HEADS, HEAD_DIM, HEAD_DIM), lambda b, t: (b, 0, 0, 0)),
        pl.BlockSpec((1, CONV_PAD, LRU_BLOCKS * LANES), lambda b, t: (b, 0, 0)),
        pl.BlockSpec((1, LRU_BLOCKS, 1, LANES), lambda b, t: (b, 0, 0, 0)),
    ] + [_const_spec(w.shape) for w in wts]
    out_shape = (
        jax.ShapeDtypeStruct((B, T, (HEADS + LRU_BLOCKS) * LANES), MXU_DTYPE),
        jax.ShapeDtypeStruct((B, CONV_PAD, QKV_GROUPS * LANES), F32),
        jax.ShapeDtypeStruct((B, HEADS, HEAD_DIM, HEAD_DIM), F32),
        jax.ShapeDtypeStruct((B, CONV_PAD, LRU_BLOCKS * LANES), F32),
        jax.ShapeDtypeStruct((B, LRU_BLOCKS, 1, LANES), F32),
    )
    out_specs = (
        pl.BlockSpec((1, TT, (HEADS + LRU_BLOCKS) * LANES), lambda b, t: (b, t, 0)),
        pl.BlockSpec((1, CONV_PAD, QKV_GROUPS * LANES), lambda b, t: (b, 0, 0)),
        pl.BlockSpec((1, HEADS, HEAD_DIM, HEAD_DIM), lambda b, t: (b, 0, 0, 0)),
        pl.BlockSpec((1, CONV_PAD, LRU_BLOCKS * LANES), lambda b, t: (b, 0, 0)),
        pl.BlockSpec((1, LRU_BLOCKS, 1, LANES), lambda b, t: (b, 0, 0, 0)),
    )
    scratch = [
        pltpu.VMEM((QKV_GROUPS, TT + CONV_PAD, LANES), F32),
        pltpu.VMEM((HEADS, TT, LANES), F32),
        pltpu.VMEM((LRU_BLOCKS, TT + CONV_PAD, LANES), F32),
        pltpu.VMEM((LRU_BLOCKS, TT, LANES), F32),
        pltpu.VMEM((3, HEADS, TT, LANES), F32),
        pltpu.VMEM((HEADS, 1, TT), F32),
        pltpu.VMEM((HEADS + LRU_BLOCKS, TT, LANES), F32),
        pltpu.VMEM((LRU_BLOCKS, 1, LANES), F32),
    ]
    return pl.pallas_call(
        kern, grid=(B, NT), in_specs=in_specs, out_specs=out_specs, out_shape=out_shape,
        scratch_shapes=scratch, name="mixer",
        compiler_params=pltpu.CompilerParams(
            dimension_semantics=("parallel", "arbitrary"), vmem_limit_bytes=VMEM_LIMIT_BYTES),
        interpret=interpret,
    )(x, gbuf0, s0, lbuf0, h0, *wts)


def _mixer_weights(w_in, gdn_conv_w, gdn_a_log, gdn_dt_bias, gdn_norm_w,
                   lru_conv_w, lru_conv_b, lru_wa, lru_ba, lru_wx, lru_bx, lru_lambda):
    qkvz = QKV_GROUPS * LANES + HEADS * LANES
    ba0 = qkvz
    lx0 = qkvz + 2 * HEADS
    wmain = jnp.concatenate([w_in[:, :qkvz], w_in[:, lx0:]], axis=1).astype(MXU_DTYPE)
    wba = jnp.pad(w_in[:, ba0:lx0], ((0, 0), (0, LANES - 2 * HEADS))).astype(MXU_DTYPE)
    gcw = gdn_conv_w.reshape(CONV_W, QKV_GROUPS, LANES).transpose(1, 0, 2)
    pad8 = (HEADS, LANES - 2 * HEADS)
    alog = jnp.pad(gdn_a_log, pad8).reshape(1, LANES)
    dtb = jnp.pad(gdn_dt_bias, pad8).reshape(1, LANES)
    gnw = gdn_norm_w.reshape(1, LANES)
    lcw = lru_conv_w.reshape(CONV_W, LRU_BLOCKS, LANES).transpose(1, 0, 2)
    lcb = lru_conv_b.reshape(LRU_BLOCKS, 1, LANES)
    wax = jnp.concatenate([lru_wa, lru_wx], axis=-1).astype(MXU_DTYPE)
    lbax = jnp.concatenate([lru_ba.reshape(LRU_BLOCKS, 1, LANES),
                            lru_bx.reshape(LRU_BLOCKS, 1, LANES)], axis=-1)
    lam = lru_lambda.reshape(LRU_BLOCKS, 1, LANES)
    return (wmain, wba, gcw, alog, dtb, gnw, lcw, lcb, wax, lbax, lam)


def _layer_norm(x, g, b):
    mu = jnp.mean(x, axis=-1, keepdims=True)
    xc = x - mu
    var = jnp.mean(xc * xc, axis=-1, keepdims=True)
    return xc * lax.rsqrt(var + LN_EPS) * g + b


def _max01(x):
    return jnp.max(jnp.max(x, axis=0, keepdims=True), axis=1, keepdims=True)


def _min01(x):
    return jnp.min(jnp.min(x, axis=0, keepdims=True), axis=1, keepdims=True)


def _sum01(x):
    return jnp.sum(jnp.sum(x, axis=0, keepdims=True), axis=1, keepdims=True)


def _post_kernel(mix_ref, x_ref, wout_ref, ln1g_ref, ln1b_ref, rwt_ref, rbias_ref, shgu_ref, shd_ref,
                 h1b_ref, base_ref, lpos_ref, gate_ref, pcnt_ref, *, tile):
    TT = tile
    hpre = DN_ALPHA * x_ref[...] + jnp.dot(mix_ref[...], wout_ref[...], preferred_element_type=F32)
    h1 = _layer_norm(hpre, ln1g_ref[...], ln1b_ref[...])
    h1b = h1.astype(MXU_DTYPE)
    h1b_ref[...] = h1b

    gu = jnp.dot(h1b, shgu_ref[...], preferred_element_type=F32)
    g_sh = gu[:, :SHARED_FF]
    hsh = (g_sh * _sigmoid(g_sh)) * gu[:, SHARED_FF:]
    base_ref[...] = DN_ALPHA * h1 + _dot(hsh, shd_ref[...])

    logits = lax.dot_general(rwt_ref[...], h1b, (((1,), (1,)), ((), ())), preferred_element_type=F32)
    scores = _sigmoid(logits)
    s3 = scores.reshape(N_GROUPS, GROUP_SIZE, TT)
    b3 = (scores + rbias_ref[...]).reshape(N_GROUPS, GROUP_SIZE, TT)
    member = lax.broadcasted_iota(I32, (N_GROUPS, GROUP_SIZE, TT), 1)
    group = lax.broadcasted_iota(I32, (N_GROUPS, GROUP_SIZE, TT), 0)
    expert = group * GROUP_SIZE + member
    neg_inf = jnp.float32(-jnp.inf)
    m1 = jnp.max(b3, axis=1, keepdims=True)
    first1 = jnp.min(jnp.where(b3 == m1, member, GROUP_SIZE), axis=1, keepdims=True)
    m2 = jnp.max(jnp.where(member == first1, neg_inf, b3), axis=1, keepdims=True)
    gs = m1 + m2
    gidx = lax.broadcasted_iota(I32, (N_GROUPS, 1, TT), 0)
    grank = jnp.zeros((N_GROUPS, 1, TT), I32)
    for go in range(N_GROUPS):
        other = gs[go:go + 1]
        beats = (other > gs) | ((other == gs) & (go < gidx))
        grank = grank + beats.astype(I32)
    masked = jnp.where(grank < TOPK_GROUPS, b3, neg_inf)
    onehots, gates = [], []
    for _ in range(TOP_K):
        mx = _max01(masked)
        first = _min01(jnp.where(masked == mx, expert, N_EXPERTS))
        hit = expert == first
        onehots.append(hit)
        gates.append(_sum01(jnp.where(hit, s3, 0.0)))
        masked = jnp.where(hit, neg_inf, masked)
    gsum = gates[0]
    for gk in gates[1:]:
        gsum = gsum + gk
    sel3 = jnp.zeros((N_GROUPS, GROUP_SIZE, TT), F32)
    for hit in onehots:
        sel3 = jnp.where(hit, 1.0, sel3)
    sel = sel3.reshape(N_EXPERTS, TT)
    tr = lax.broadcasted_iota(I32, (TT, TT), 0)
    tc = lax.broadcasted_iota(I32, (TT, TT), 1)
    rank = _dot(sel, (tr < tc).astype(F32))
    cnt = jnp.sum(sel, axis=1, keepdims=True)
    pcnt = jnp.floor((cnt + (BF16_ROWS - 1)) * (1.0 / BF16_ROWS)) * BF16_ROWS
    er = lax.broadcasted_iota(I32, (N_EXPERTS, N_EXPERTS), 0)
    ec = lax.broadcasted_iota(I32, (N_EXPERTS, N_EXPERTS), 1)
    pcnt_b = jnp.broadcast_to(pcnt, (N_EXPERTS, LANES))
    loff = _dot_f32((ec < er).astype(F32), pcnt_b)[:, 0:1]
    lpos3 = (loff + rank).reshape(N_GROUPS, GROUP_SIZE, TT)
    for k in range(TOP_K):
        lp = _sum01(jnp.where(onehots[k], lpos3, 0.0))
        lpos_ref[0, k:k + 1, :] = lp.reshape(1, TT).astype(I32)
        gate_ref[0, k:k + 1, :] = (gates[k] / gsum * ROUTED_SCALE).reshape(1, TT)
    pcnt_ref[0] = pcnt_b.astype(I32)


def _post(mix_in, x2, wts, *, interpret=False):
    N = x2.shape[0]
    TT = TOK_TILE
    assert N % TT == 0
    nt = N // TT
    tok = lambda w: pl.BlockSpec((TT, w), lambda i: (i, 0))
    const = lambda a: pl.BlockSpec(a.shape, lambda i, _n=a.ndim: (0,) * _n)
    out_shape = (
        jax.ShapeDtypeStruct((N, D_MODEL), MXU_DTYPE),
        jax.ShapeDtypeStruct((N, D_MODEL), F32),
        jax.ShapeDtypeStruct((nt, TOP_K, TT), I32),
        jax.ShapeDtypeStruct((nt, TOP_K, TT), F32),
        jax.ShapeDtypeStruct((nt, N_EXPERTS, LANES), I32),
    )
    out_specs = (
        tok(D_MODEL), tok(D_MODEL),
        pl.BlockSpec((1, TOP_K, TT), lambda i: (i, 0, 0)),
        pl.BlockSpec((1, TOP_K, TT), lambda i: (i, 0, 0)),
        pl.BlockSpec((1, N_EXPERTS, LANES), lambda i: (i, 0, 0)),
    )
    return pl.pallas_call(
        functools.partial(_post_kernel, tile=TT), grid=(nt,),
        in_specs=[tok(mix_in.shape[1]), tok(D_MODEL)] + [const(w) for w in wts],
        out_specs=out_specs, out_shape=out_shape, name="post",
        compiler_params=pltpu.CompilerParams(
            dimension_semantics=("parallel",), vmem_limit_bytes=VMEM_LIMIT_BYTES),
        interpret=interpret,
    )(mix_in, x2, *wts)


def _post_weights(w_out, ln1_g, ln1_b, router_w, router_bias, sh_w_gate, sh_w_up, sh_w_down):
    return (w_out.astype(MXU_DTYPE), ln1_g.reshape(1, D_MODEL), ln1_b.reshape(1, D_MODEL),
            router_w.T.astype(MXU_DTYPE), router_bias.reshape(N_EXPERTS, 1),
            jnp.concatenate([sh_w_gate, sh_w_up], axis=1).astype(MXU_DTYPE), sh_w_down.astype(MXU_DTYPE))


SORT_CHUNK = 512


def _local_rows(tile):
    worst = tile * TOP_K + N_EXPERTS * (BF16_ROWS - 1)
    return -(-worst // SORT_CHUNK) * SORT_CHUNK


def _run_copy(loc_ref, hbm_ref, sem, loc_row, hbm_row, to_hbm):
    loc = loc_ref.at[pl.ds(pl.multiple_of(loc_row, BF16_ROWS), BF16_ROWS), :]
    hbm = hbm_ref.at[pl.ds(pl.multiple_of(hbm_row, BF16_ROWS), BF16_ROWS), :]
    return pltpu.make_async_copy(loc, hbm, sem) if to_hbm else pltpu.make_async_copy(hbm, loc, sem)


def _move_runs(tab_ref, loc_ref, hbm_ref, sem, to_hbm):
    def per_expert(e, off):
        n = tab_ref[0, 0, e] // BF16_ROWS
        dst0 = tab_ref[0, 0, N_EXPERTS + e]

        def per_granule(j, c):
            _run_copy(loc_ref, hbm_ref, sem, off + j * BF16_ROWS, dst0 + j * BF16_ROWS, to_hbm).start()
            return c

        lax.fori_loop(0, n, per_granule, 0)
        return off + n * BF16_ROWS

    total = lax.fori_loop(0, N_EXPERTS, per_expert, 0)

    def wait_one(j, c):
        _run_copy(loc_ref, hbm_ref, sem, 0, 0, to_hbm).wait()
        return c

    lax.fori_loop(0, total // BF16_ROWS, wait_one, 0)


def _dispatch_kernel(tab_ref, tail_ref, h1b_ref, lpos_ref, xs_ref, xloc, zbuf, sem, *, tile):
    TT = tile
    LR = _local_rows(TT)
    xb = h1b_ref[...]
    lp = lpos_ref[0]
    for rc in range(LR // SORT_CHUNK):
        rows = lax.broadcasted_iota(I32, (SORT_CHUNK, TT), 0) + rc * SORT_CHUNK
        p = jnp.zeros((SORT_CHUNK, TT), F32)
        for k in range(TOP_K):
            p = jnp.where(rows == lp[k:k + 1, :], 1.0, p)
        xl = jnp.dot(p.astype(MXU_DTYPE), xb, preferred_element_type=F32)
        xloc[rc * SORT_CHUNK:(rc + 1) * SORT_CHUNK, :] = xl.astype(xloc.dtype)
    _move_runs(tab_ref, xloc, xs_ref, sem, True)

    @pl.when(pl.program_id(0) == pl.num_programs(0) - 1)
    def _zero_tails():
        zbuf[...] = jnp.zeros_like(zbuf)

        def per_expert(e, tot):
            n = tail_ref[0, 0, N_EXPERTS + e]
            start = tail_ref[0, 0, e]

            def per_granule(j, c):
                _run_copy(zbuf, xs_ref, sem, 0, start + j * BF16_ROWS, True).start()
                return c

            lax.fori_loop(0, n, per_granule, 0)
            return tot + n

        total = lax.fori_loop(0, N_EXPERTS, per_expert, 0)

        def wait_one(j, c):
            _run_copy(zbuf, xs_ref, sem, 0, 0, True).wait()
            return c

        lax.fori_loop(0, total, wait_one, 0)


def _dispatch(h1b, lpos, tab, tail, n_rows, *, interpret=False):
    N = h1b.shape[0]
    TT = TOK_TILE
    nt = N // TT
    return pl.pallas_call(
        functools.partial(_dispatch_kernel, tile=TT), grid=(nt,),
        in_specs=[
            pl.BlockSpec((1, 1, 2 * N_EXPERTS), lambda i: (i, 0, 0), memory_space=pltpu.SMEM),
            pl.BlockSpec((1, 1, 2 * N_EXPERTS), lambda i: (0, 0, 0), memory_space=pltpu.SMEM),
            pl.BlockSpec((TT, D_MODEL), lambda i: (i, 0)),
            pl.BlockSpec((1, TOP_K, TT), lambda i: (i, 0, 0)),
        ],
        out_specs=pl.BlockSpec(memory_space=pl.ANY),
        out_shape=jax.ShapeDtypeStruct((n_rows, D_MODEL), MXU_DTYPE),
        scratch_shapes=[pltpu.VMEM((_local_rows(TT), D_MODEL), MXU_DTYPE),
                        pltpu.VMEM((BF16_ROWS, D_MODEL), MXU_DTYPE),
                        pltpu.SemaphoreType.DMA(())],
        name="dispatch",
        compiler_params=pltpu.CompilerParams(
            dimension_semantics=("arbitrary",), vmem_limit_bytes=VMEM_LIMIT_BYTES),
        interpret=interpret,
    )(tab, tail, h1b, lpos)


def _expert_kernel(blk_e_ref, nact_ref, x_ref, wgu_ref, wd_ref, y_ref):
    j = pl.program_id(0)

    @pl.when(j < nact_ref[0])
    def _compute():
        gu = jnp.dot(x_ref[...], wgu_ref[0], preferred_element_type=F32)
        g = gu[:, :EXPERT_FF]
        h = (g * _sigmoid(g)) * gu[:, EXPERT_FF:]
        y_ref[...] = _dot(h, wd_ref[0]).astype(y_ref.dtype)

    @pl.when(j >= nact_ref[0])
    def _idle():
        y_ref[...] = jnp.zeros_like(y_ref)


def _experts(xs, wgu, wd, blk_e, nact, *, interpret=False):
    BM = ROW_BLOCK
    nblk = xs.shape[0] // BM
    grid_spec = pltpu.PrefetchScalarGridSpec(
        num_scalar_prefetch=2, grid=(nblk,),
        in_specs=[
            pl.BlockSpec((BM, D_MODEL), lambda j, be, na: (jnp.minimum(j, na[0] - 1), 0)),
            pl.BlockSpec((1, D_MODEL, 2 * EXPERT_FF), lambda j, be, na: (be[j], 0, 0)),
            pl.BlockSpec((1, EXPERT_FF, D_MODEL), lambda j, be, na: (be[j], 0, 0)),
        ],
        out_specs=pl.BlockSpec((BM, D_MODEL), lambda j, be, na: (jnp.where(j < na[0], j, nblk), 0)),
    )
    return pl.pallas_call(
        _expert_kernel, grid_spec=grid_spec,
        out_shape=jax.ShapeDtypeStruct(((nblk + 1) * BM, D_MODEL), MXU_DTYPE), name="experts",
        compiler_params=pltpu.CompilerParams(
            dimension_semantics=("arbitrary",), vmem_limit_bytes=VMEM_LIMIT_BYTES),
        interpret=interpret,
    )(blk_e, nact, xs, wgu, wd)


def _combine_kernel(tab_ref, lpos_ref, gate_ref, base_ref, ln2g_ref, ln2b_ref, ys_ref, out_ref, yloc, sem,
                    *, tile):
    TT = tile
    LR = _local_rows(TT)

    @pl.when(pl.program_id(0) == 0)
    def _init():
        yloc[...] = jnp.zeros_like(yloc)

    _move_runs(tab_ref, yloc, ys_ref, sem, False)
    lp = lpos_ref[0]
    gt = gate_ref[0]
    acc = base_ref[...]
    for rc in range(LR // SORT_CHUNK):
        rows = lax.broadcasted_iota(I32, (SORT_CHUNK, TT), 0) + rc * SORT_CHUNK
        w = jnp.zeros((SORT_CHUNK, TT), F32)
        for k in range(TOP_K):
            w = jnp.where(rows == lp[k:k + 1, :], gt[k:k + 1, :], w)
        acc = acc + _dot_tn(w, yloc[rc * SORT_CHUNK:(rc + 1) * SORT_CHUNK, :])
    out_ref[...] = _layer_norm(acc, ln2g_ref[...], ln2b_ref[...])


def _combine(ys, lpos, gate, base, ln2g, ln2b, tab, *, interpret=False):
    N = base.shape[0]
    TT = TOK_TILE
    nt = N // TT
    return pl.pallas_call(
        functools.partial(_combine_kernel, tile=TT), grid=(nt,),
        in_specs=[
            pl.BlockSpec((1, 1, 2 * N_EXPERTS), lambda i: (i, 0, 0), memory_space=pltpu.SMEM),
            pl.BlockSpec((1, TOP_K, TT), lambda i: (i, 0, 0)),
            pl.BlockSpec((1, TOP_K, TT), lambda i: (i, 0, 0)),
            pl.BlockSpec((TT, D_MODEL), lambda i: (i, 0)),
            pl.BlockSpec((1, D_MODEL), lambda i: (0, 0)),
            pl.BlockSpec((1, D_MODEL), lambda i: (0, 0)),
            pl.BlockSpec(memory_space=pl.ANY),
        ],
        out_specs=pl.BlockSpec((TT, D_MODEL), lambda i: (i, 0)),
        out_shape=jax.ShapeDtypeStruct((N, D_MODEL), F32),
        scratch_shapes=[pltpu.VMEM((_local_rows(TT), D_MODEL), MXU_DTYPE), pltpu.SemaphoreType.DMA(())],
        name="combine",
        compiler_params=pltpu.CompilerParams(
            dimension_semantics=("arbitrary",), vmem_limit_bytes=VMEM_LIMIT_BYTES),
        interpret=interpret,
    )(tab, lpos, gate, base, ln2g, ln2b, ys)


def _moe_weights(exp_w_gate, exp_w_up, exp_w_down, ln2_g, ln2_b):
    return (jnp.concatenate([exp_w_gate, exp_w_up], axis=-1).astype(MXU_DTYPE), exp_w_down.astype(MXU_DTYPE),
            ln2_g.reshape(1, D_MODEL), ln2_b.reshape(1, D_MODEL))


def _sorted_layout(pcnt):
    nt = pcnt.shape[0]
    run_len = jnp.sum(pcnt, axis=0)
    region = (run_len + ROW_BLOCK - 1) // ROW_BLOCK * ROW_BLOCK
    region_end = jnp.cumsum(region)
    region_start = region_end - region
    gbase = region_start[None, :] + jnp.cumsum(pcnt, axis=0) - pcnt
    tab = jnp.concatenate([pcnt, gbase], axis=1).reshape(nt, 1, 2 * N_EXPERTS).astype(I32)
    tail = jnp.concatenate([region_start + run_len, (region - run_len) // BF16_ROWS]
                           ).reshape(1, 1, 2 * N_EXPERTS).astype(I32)
    max_rows = nt * (TOK_TILE * TOP_K + N_EXPERTS * (BF16_ROWS - 1)) + N_EXPERTS * (ROW_BLOCK - BF16_ROWS)
    nblk = -(-max_rows // ROW_BLOCK)
    blk_e = jnp.minimum(jnp.searchsorted(region_end, jnp.arange(nblk, dtype=I32) * ROW_BLOCK, side='right'),
                        N_EXPERTS - 1).astype(I32)
    nact = (region_end[-1] // ROW_BLOCK).reshape(1).astype(I32)
    return tab, tail, blk_e, nact, nblk * ROW_BLOCK


def _layer(x, gbuf0, s0, lbuf0, h0, mix_w, post_w, moe_w, *, chunk, reset_first, interpret=False):
    B, T, D = x.shape
    mix_in, gbuf, s_new, lbuf, h_new = _mixer(x, gbuf0, s0, lbuf0, h0, mix_w, chunk=chunk,
                                              reset_first=reset_first, interpret=interpret)
    N = B * T
    h1b, base, lpos, gate, pcnt = _post(mix_in.reshape(N, -1), x.reshape(N, D), post_w, interpret=interpret)
    tab, tail, blk_e, nact, n_rows = _sorted_layout(pcnt[:, :, 0])
    wgu, wd, ln2g, ln2b = moe_w
    xs = _dispatch(h1b, lpos, tab, tail, n_rows, interpret=interpret)
    ys = _experts(xs, wgu, wd, blk_e, nact, interpret=interpret)
    y = _combine(ys, lpos, gate, base, ln2g, ln2b, tab, interpret=interpret)
    keep = CONV_PAD - (CONV_W - 1)
    states = (gbuf[:, keep:], s_new, lbuf[:, keep:], h_new.reshape(B, D))
    return y.reshape(B, T, D), states


def kernel(x_prompt, x_sample, state_gdn_conv, state_gdn, state_lru_conv, state_lru, w_in, gdn_conv_w, gdn_a_log, gdn_dt_bias, gdn_norm_w, lru_conv_w, lru_conv_b, lru_wa, lru_ba, lru_wx, lru_bx, lru_lambda, w_out, ln1_g, ln1_b, router_w, router_bias, exp_w_gate, exp_w_up, exp_w_down, sh_w_gate, sh_w_up, sh_w_down, ln2_g, ln2_b):
    mix_w = _mixer_weights(w_in[0], gdn_conv_w[0], gdn_a_log[0], gdn_dt_bias[0], gdn_norm_w[0], lru_conv_w[0],
                           lru_conv_b[0], lru_wa[0], lru_ba[0], lru_wx[0], lru_bx[0], lru_lambda[0])
    post_w = _post_weights(w_out[0], ln1_g[0], ln1_b[0], router_w[0], router_bias[0],
                           sh_w_gate[0], sh_w_up[0], sh_w_down[0])
    moe_w = _moe_weights(exp_w_gate[0], exp_w_up[0], exp_w_down[0], ln2_g[0], ln2_b[0])
    bp, bs = x_prompt.shape[0], x_sample.shape[0]
    zeros = lambda *s: jnp.zeros(s, F32)
    pad_rows = lambda a: jnp.pad(a, ((0, 0), (CONV_PAD - (CONV_W - 1), 0), (0, 0)))
    yp, sp = _layer(x_prompt, zeros(bp, CONV_PAD, QKV_GROUPS * LANES), zeros(bp, HEADS, HEAD_DIM, HEAD_DIM),
                    zeros(bp, CONV_PAD, D_MODEL), zeros(bp, LRU_BLOCKS, 1, LANES),
                    mix_w, post_w, moe_w, chunk=64, reset_first=True)
    ys, ss = _layer(x_sample, pad_rows(state_gdn_conv[0]), state_gdn[0], pad_rows(state_lru_conv[0]),
                    state_lru[0].reshape(bs, LRU_BLOCKS, 1, LANES),
                    mix_w, post_w, moe_w, chunk=x_sample.shape[1], reset_first=False)
    return (yp, ys) + tuple(a[None] for a in sp) + tuple(a[None] for a in ss)
```

```python
import functools
import math

import jax
import jax.numpy as jnp
from jax import lax
from jax.experimental import pallas as pl
from jax.experimental.pallas import tpu as pltpu

F32 = jnp.float32
I32 = jnp.int32
MXU_DTYPE = jnp.bfloat16

D_MODEL = 1024
HEADS = 8
HEAD_DIM = 128
QKV_GROUPS = 3 * HEADS
LRU_BLOCKS = 8
CONV_W = 4
CONV_PAD = 8
N_EXPERTS = 64
N_GROUPS = 8
GROUP_SIZE = N_EXPERTS // N_GROUPS
TOPK_GROUPS = 4
TOP_K = 8
EXPERT_FF = 256
SHARED_FF = 256
ROUTED_SCALE = 2.5
LRU_C = 8.0
DEPTH = 1
DN_ALPHA = (2.0 * DEPTH) ** 0.25
LN_EPS = 1e-5
RMS_EPS = 1e-6
L2_EPS = 1e-6

LANES = 128
SUBLANES = 8
BF16_ROWS = 16
VMEM_LIMIT_BYTES = 56 * 1024 * 1024

MIX_TILE = 256
HEAD_GROUP = 4
TOK_TILE = 256
ROW_BLOCK = 512


def _dot(a, b):
    return jnp.dot(a.astype(MXU_DTYPE), b.astype(MXU_DTYPE), preferred_element_type=F32)


def _dot_nt(a, b):
    return lax.dot_general(a.astype(MXU_DTYPE), b.astype(MXU_DTYPE),
                           (((1,), (1,)), ((), ())), preferred_element_type=F32)


def _dot_tn(a, b):
    return lax.dot_general(a.astype(MXU_DTYPE), b.astype(MXU_DTYPE),
                           (((0,), (0,)), ((), ())), preferred_element_type=F32)


def _dot_f32(a, b):
    return jnp.dot(a, b, precision=lax.Precision.HIGHEST, preferred_element_type=F32)


def _sigmoid(x):
    return 1.0 / (1.0 + jnp.exp(-x))


def _softplus(x):
    return jnp.maximum(x, 0.0) + jnp.log1p(jnp.exp(-jnp.abs(x)))


def _widen(col, width):
    if width <= LANES:
        return col[:, :width]
    return jnp.concatenate([col] * (width // LANES), axis=1)


def _mixer_kernel(x_ref, gbuf0_ref, s0_ref, lbuf0_ref, h0_ref,
                  wmain_ref, wba_ref, gcw_ref, alog_ref, dtb_ref, gnw_ref,
                  lcw_ref, lcb_ref, wax_ref, lbax_ref, lam_ref,
                  mix_ref, gbuf_ref, s_ref, lbuf_ref, h_ref,
                  qkv_s, z_s, lx_s, ly_s, col_s, row_s, o_s, hst_s,
                  *, tile, chunk, reset_first):
    TT, C = tile, chunk
    NC = TT // C
    t = pl.program_id(1)

    @pl.when(t == 0)
    def _load_state():
        for g in range(QKV_GROUPS):
            qkv_s[g, 0:CONV_PAD, :] = gbuf0_ref[0, :, g * LANES:(g + 1) * LANES]
        for k in range(LRU_BLOCKS):
            lx_s[k, 0:CONV_PAD, :] = lbuf0_ref[0, :, k * LANES:(k + 1) * LANES]
        s_ref[...] = s0_ref[...]
        hst_s[...] = h0_ref[0]

    xb = x_ref[0].astype(MXU_DTYPE)
    for gp in range(QKV_GROUPS // 2):
        res = jnp.dot(xb, wmain_ref[:, gp * 256:(gp + 1) * 256], preferred_element_type=F32)
        qkv_s[2 * gp, CONV_PAD:CONV_PAD + TT, :] = res[:, :LANES]
        qkv_s[2 * gp + 1, CONV_PAD:CONV_PAD + TT, :] = res[:, LANES:]
    col0 = QKV_GROUPS * LANES
    for gp in range(HEADS // 2):
        res = jnp.dot(xb, wmain_ref[:, col0 + gp * 256:col0 + (gp + 1) * 256], preferred_element_type=F32)
        z_s[2 * gp] = res[:, :LANES]
        z_s[2 * gp + 1] = res[:, LANES:]
    col0 += HEADS * LANES
    for gp in range(LRU_BLOCKS // 2):
        res = jnp.dot(xb, wmain_ref[:, col0 + gp * 256:col0 + (gp + 1) * 256], preferred_element_type=F32)
        lx_s[2 * gp, CONV_PAD:CONV_PAD + TT, :] = res[:, :LANES]
        lx_s[2 * gp + 1, CONV_PAD:CONV_PAD + TT, :] = res[:, LANES:]
    col0 += LRU_BLOCKS * LANES
    for gp in range(LRU_BLOCKS // 2):
        res = jnp.dot(xb, wmain_ref[:, col0 + gp * 256:col0 + (gp + 1) * 256], preferred_element_type=F32)
        ly_s[2 * gp] = res[:, :LANES]
        ly_s[2 * gp + 1] = res[:, LANES:]
    ba = jnp.dot(xb, wba_ref[...], preferred_element_type=F32)

    for g in range(QKV_GROUPS):
        gbuf_ref[0, :, g * LANES:(g + 1) * LANES] = qkv_s[g, TT:TT + CONV_PAD, :]
    for k in range(LRU_BLOCKS):
        lbuf_ref[0, :, k * LANES:(k + 1) * LANES] = lx_s[k, TT:TT + CONV_PAD, :]

    row = lax.broadcasted_iota(I32, (TT, TT), 0)
    colm = lax.broadcasted_iota(I32, (TT, TT), 1)
    same_chunk = (row // C) == (colm // C)
    causal = same_chunk & (row >= colm)
    strict = same_chunk & (row > colm)
    beta_all = _sigmoid(ba)
    g_all = -jnp.exp(alog_ref[...]) * _softplus(ba + dtb_ref[...])
    gc_all = _dot_f32(causal.astype(F32), g_all)
    gt_all = _dot_f32(same_chunk.astype(F32), g_all)
    gc_t = gc_all.T
    for h in range(HEADS):
        col_s[0, h] = jnp.broadcast_to(beta_all[:, h:h + 1], (TT, LANES))
        col_s[1, h] = jnp.broadcast_to(gc_all[:, HEADS + h:HEADS + h + 1], (TT, LANES))
        col_s[2, h] = jnp.broadcast_to(gt_all[:, HEADS + h:HEADS + h + 1], (TT, LANES))
        row_s[h] = gc_t[HEADS + h:HEADS + h + 1, :]

    def conv_silu(g):
        w = gcw_ref[g]
        acc = qkv_s[g, pl.ds(CONV_PAD - 3, TT), :] * w[0:1]
        acc = acc + qkv_s[g, pl.ds(CONV_PAD - 2, TT), :] * w[1:2]
        acc = acc + qkv_s[g, pl.ds(CONV_PAD - 1, TT), :] * w[2:3]
        acc = acc + qkv_s[g, pl.ds(CONV_PAD, TT), :] * w[3:4]
        return acc * _sigmoid(acc)

    n_levels = int(math.log2(C))
    cat = lambda parts, axis: parts[0] if len(parts) == 1 else jnp.concatenate(parts, axis=axis)

    def head_group_body(hg, carry):
        hs = [hg * HEAD_GROUP + i for i in range(HEAD_GROUP)]
        R = range(HEAD_GROUP)
        q = [conv_silu(h) for h in hs]
        k = [conv_silu(HEADS + h) for h in hs]
        v = [conv_silu(2 * HEADS + h) for h in hs]
        q = [x * lax.rsqrt(jnp.sum(x * x, axis=-1, keepdims=True) + L2_EPS) * (HEAD_DIM ** -0.5) for x in q]
        k = [x * lax.rsqrt(jnp.sum(x * x, axis=-1, keepdims=True) + L2_EPS) for x in k]
        beta = [col_s[0, h] for h in hs]
        gcol = [col_s[1, h] for h in hs]
        gtot = [col_s[2, h] for h in hs]
        decay = [jnp.where(causal, jnp.exp(_widen(gcol[i], TT) - row_s[hs[i]]), 0.0) for i in R]
        eg = [jnp.exp(g) for g in gcol]
        kb = [k[i] * beta[i] for i in R]
        qks = [_dot_nt(jnp.concatenate([kb[i], q[i]], axis=0), k[i]) for i in R]
        nmat = [jnp.where(strict, -(qks[i][:TT] * decay[i]), 0.0) for i in R]
        qk = [qks[i][TT:] * decay[i] for i in R]
        tm = nmat
        if n_levels > 1:
            npow = [_dot(n, n) for n in nmat]
            for j in range(1, n_levels):
                if j < n_levels - 1:
                    r2 = [_dot(jnp.concatenate([tm[i], npow[i]], axis=0), npow[i]) for i in R]
                    tm = [tm[i] + npow[i] + r2[i][:TT] for i in R]
                    npow = [r2[i][TT:] for i in R]
                else:
                    tm = [tm[i] + npow[i] + _dot(tm[i], npow[i]) for i in R]
        rhs = [jnp.concatenate([kb[i] * eg[i], v[i] * beta[i]], axis=1) for i in R]
        wu = [rhs[i] + _dot(tm[i], rhs[i]) for i in R]
        qd = [q[i] * eg[i] for i in R]
        kd = [k[i] * jnp.exp(gtot[i] - gcol[i]) for i in R]
        egt = [jnp.exp(g) for g in gtot]
        kwu = [[_dot_tn(kd[i][c * C:(c + 1) * C], wu[i][c * C:(c + 1) * C]) for c in range(NC)] for i in R]
        state = [s_ref[0, h] for h in hs]
        starts = [[] for _ in R]
        for c in range(NC):
            for i in R:
                starts[i].append(state[i])
                m = kwu[i][c]
                state[i] = (state[i] * egt[i][c * C:c * C + 1, :] + m[:, LANES:]) - _dot(m[:, :LANES], state[i])
        for i in R:
            s_ref[0, hs[i]] = state[i]
        for i in R:
            wq = [_dot(jnp.concatenate([wu[i][c * C:(c + 1) * C, :LANES], qd[i][c * C:(c + 1) * C]], axis=0),
                       starts[i][c]) for c in range(NC)]
            vn = cat([wu[i][c * C:(c + 1) * C, LANES:] - wq[c][:C] for c in range(NC)], 0)
            o = cat([wq[c][C:] for c in range(NC)], 0) + _dot(qk[i], vn)
            o = o * lax.rsqrt(jnp.mean(o * o, axis=-1, keepdims=True) + RMS_EPS) * gnw_ref[...]
            zz = z_s[hs[i]]
            o_s[hs[i]] = o * (zz * _sigmoid(zz))
        return carry

    lax.fori_loop(0, HEADS // HEAD_GROUP, head_group_body, 0)

    G = TT // SUBLANES
    sub = lax.broadcasted_iota(I32, (G, SUBLANES, LANES), 1)
    rowi = lax.broadcasted_iota(I32, (TT, LANES), 0)
    first_row = rowi == 0

    def lru_body(kb_, carry):
        w = lcw_ref[kb_]
        xc = lx_s[kb_, pl.ds(CONV_PAD - 3, TT), :] * w[0:1]
        xc = xc + lx_s[kb_, pl.ds(CONV_PAD - 2, TT), :] * w[1:2]
        xc = xc + lx_s[kb_, pl.ds(CONV_PAD - 1, TT), :] * w[2:3]
        xc = xc + lx_s[kb_, pl.ds(CONV_PAD, TT), :] * w[3:4]
        xc = xc + lcb_ref[kb_]
        gates = _sigmoid(_dot(xc, wax_ref[kb_]) + lbax_ref[kb_])
        r = gates[:, :LANES]
        gi = gates[:, LANES:]
        log_a = (LRU_C * r) * (-_softplus(-lam_ref[kb_]))
        a = jnp.exp(log_a)
        th = jnp.tanh(log_a)
        mult = jnp.sqrt(-2.0 * th / (1.0 - th))
        if reset_first:
            mult = jnp.where(first_row & (t == 0), 1.0, mult)
        hprev = hst_s[kb_]
        b = mult * gi * xc + jnp.where(first_row, a * hprev, 0.0)
        a3 = a.reshape(G, SUBLANES, LANES)
        b3 = b.reshape(G, SUBLANES, LANES)
        for s in (1, 2, 4):
            a_sh = jnp.where(sub >= s, pltpu.roll(a3, s, 1), 1.0)
            b_sh = jnp.where(sub >= s, pltpu.roll(b3, s, 1), 0.0)
            b3 = a3 * b_sh + b3
            a3 = a3 * a_sh
        hs = []
        hc = jnp.zeros((1, LANES), F32)
        for gidx in range(G):
            hg = b3[gidx] if gidx == 0 else a3[gidx] * hc + b3[gidx]
            hc = hg[SUBLANES - 1:SUBLANES, :]
            hs.append(hg)
        hall = jnp.concatenate(hs, axis=0)
        hst_s[kb_] = hc
        o_s[HEADS + kb_] = hall * jax.nn.gelu(ly_s[kb_])
        return carry

    lax.fori_loop(0, LRU_BLOCKS, lru_body, 0)

    for g in range(HEADS + LRU_BLOCKS):
        mix_ref[0, :, g * LANES:(g + 1) * LANES] = o_s[g].astype(mix_ref.dtype)
    h_ref[0] = hst_s[...]
    for g in range(QKV_GROUPS):
        qkv_s[g, 0:CONV_PAD, :] = qkv_s[g, TT:TT + CONV_PAD, :]
    for k in range(LRU_BLOCKS):
        lx_s[k, 0:CONV_PAD, :] = lx_s[k, TT:TT + CONV_PAD, :]


def _const_spec(shape):
    nd = len(shape)
    return pl.BlockSpec(shape, lambda b, t, _n=nd: (0,) * _n)


def _mixer(x, gbuf0, s0, lbuf0, h0, wts, *, chunk, reset_first, interpret=False):
    B, T, _ = x.shape
    TT = min(MIX_TILE, T)
    assert T % TT == 0 and TT % chunk == 0 and TT % SUBLANES == 0
    NT = T // TT
    kern = functools.partial(_mixer_kernel, tile=TT, chunk=chunk, reset_first=reset_first)
    in_specs = [
        pl.BlockSpec((1, TT, D_MODEL), lambda b, t: (b, t, 0)),
        pl.BlockSpec((1, CONV_PAD, QKV_GROUPS * LANES), lambda b, t: (b, 0, 0)),
        pl.BlockSpec((1, HEADS, HEAD_DIM, HEAD_DIM), lambda b, t: (b, 0, 0, 0)),
        pl.BlockSpec((1, CONV_PAD, LRU_BLOCKS * LANES), lambda b, t: (b, 0, 0)),
        pl.BlockSpec((1, LRU_BLOCKS, 1, LANES), lambda b, t: (b, 0, 0, 0)),
    ] + [_const_spec(w.shape) for w in wts]
    out_shape = (
        jax.ShapeDtypeStruct((B, T, (HEADS + LRU_BLOCKS) * LANES), MXU_DTYPE),
        jax.ShapeDtypeStruct((B, CONV_PAD, QKV_GROUPS * LANES), F32),
        jax.ShapeDtypeStruct((B, HEADS, HEAD_DIM, HEAD_DIM), F32),
        jax.ShapeDtypeStruct((B, CONV_PAD, LRU_BLOCKS * LANES), F32),
        jax.ShapeDtypeStruct((B, LRU_BLOCKS, 1, LANES), F32),
    )
    out_specs = (
        pl.BlockSpec((1, TT, (HEADS + LRU_BLOCKS) * LANES), lambda b, t: (b, t, 0)),
        pl.BlockSpec((1, CONV_PAD, QKV_GROUPS * LANES), lambda b, t: (b, 0, 0)),
        pl.BlockSpec((1, HEADS, HEAD_DIM, HEAD_DIM), lambda b, t: (b, 0, 0, 0)),
        pl.BlockSpec((1, CONV_PAD, LRU_BLOCKS * LANES), lambda b, t: (b, 0, 0)),
        pl.BlockSpec((1, LRU_BLOCKS, 1, LANES), lambda b, t: (b, 0, 0, 0)),
    )
    scratch = [
        pltpu.VMEM((QKV_GROUPS, TT + CONV_PAD, LANES), F32),
        pltpu.VMEM((HEADS, TT, LANES), F32),
        pltpu.VMEM((LRU_BLOCKS, TT + CONV_PAD, LANES), F32),
        pltpu.VMEM((LRU_BLOCKS, TT, LANES), F32),
        pltpu.VMEM((3, HEADS, TT, LANES), F32),
        pltpu.VMEM((HEADS, 1, TT), F32),
        pltpu.VMEM((HEADS + LRU_BLOCKS, TT, LANES), F32),
        pltpu.VMEM((LRU_BLOCKS, 1, LANES), F32),
    ]
    return pl.pallas_call(
        kern, grid=(B, NT), in_specs=in_specs, out_specs=out_specs, out_shape=out_shape,
        scratch_shapes=scratch, name="mixer",
        compiler_params=pltpu.CompilerParams(
            dimension_semantics=("parallel", "arbitrary"), vmem_limit_bytes=VMEM_LIMIT_BYTES),
        interpret=interpret,
    )(x, gbuf0, s0, lbuf0, h0, *wts)


def _mixer_weights(w_in, gdn_conv_w, gdn_a_log, gdn_dt_bias, gdn_norm_w,
                   lru_conv_w, lru_conv_b, lru_wa, lru_ba, lru_wx, lru_bx, lru_lambda):
    qkvz = QKV_GROUPS * LANES + HEADS * LANES
    ba0 = qkvz
    lx0 = qkvz + 2 * HEADS
    wmain = jnp.concatenate([w_in[:, :qkvz], w_in[:, lx0:]], axis=1).astype(MXU_DTYPE)
    wba = jnp.pad(w_in[:, ba0:lx0], ((0, 0), (0, LANES - 2 * HEADS))).astype(MXU_DTYPE)
    gcw = gdn_conv_w.reshape(CONV_W, QKV_GROUPS, LANES).transpose(1, 0, 2)
    pad8 = (HEADS, LANES - 2 * HEADS)
    alog = jnp.pad(gdn_a_log, pad8).reshape(1, LANES)
    dtb = jnp.pad(gdn_dt_bias, pad8).reshape(1, LANES)
    gnw = gdn_norm_w.reshape(1, LANES)
    lcw = lru_conv_w.reshape(CONV_W, LRU_BLOCKS, LANES).transpose(1, 0, 2)
    lcb = lru_conv_b.reshape(LRU_BLOCKS, 1, LANES)
    wax = jnp.concatenate([lru_wa, lru_wx], axis=-1).astype(MXU_DTYPE)
    lbax = jnp.concatenate([lru_ba.reshape(LRU_BLOCKS, 1, LANES),
                            lru_bx.reshape(LRU_BLOCKS, 1, LANES)], axis=-1)
    lam = lru_lambda.reshape(LRU_BLOCKS, 1, LANES)
    return (wmain, wba, gcw, alog, dtb, gnw, lcw, lcb, wax, lbax, lam)


def _layer_norm(x, g, b):
    mu = jnp.mean(x, axis=-1, keepdims=True)
    xc = x - mu
    var = jnp.mean(xc * xc, axis=-1, keepdims=True)
    return xc * lax.rsqrt(var + LN_EPS) * g + b


def _max01(x):
    return jnp.max(jnp.max(x, axis=0, keepdims=True), axis=1, keepdims=True)


def _min01(x):
    return jnp.min(jnp.min(x, axis=0, keepdims=True), axis=1, keepdims=True)


def _sum01(x):
    return jnp.sum(jnp.sum(x, axis=0, keepdims=True), axis=1, keepdims=True)


def _post_kernel(mix_ref, x_ref, wout_ref, ln1g_ref, ln1b_ref, rwt_ref, rbias_ref, shgu_ref, shd_ref,
                 h1b_ref, base_ref, lpos_ref, gate_ref, pcnt_ref, *, tile):
    TT = tile
    hpre = DN_ALPHA * x_ref[...] + jnp.dot(mix_ref[...], wout_ref[...], preferred_element_type=F32)
    h1 = _layer_norm(hpre, ln1g_ref[...], ln1b_ref[...])
    h1b = h1.astype(MXU_DTYPE)
    h1b_ref[...] = h1b

    gu = jnp.dot(h1b, shgu_ref[...], preferred_element_type=F32)
    g_sh = gu[:, :SHARED_FF]
    hsh = (g_sh * _sigmoid(g_sh)) * gu[:, SHARED_FF:]
    base_ref[...] = DN_ALPHA * h1 + _dot(hsh, shd_ref[...])

    logits = lax.dot_general(rwt_ref[...], h1b, (((1,), (1,)), ((), ())), preferred_element_type=F32)
    scores = _sigmoid(logits)
    s3 = scores.reshape(N_GROUPS, GROUP_SIZE, TT)
    b3 = (scores + rbias_ref[...]).reshape(N_GROUPS, GROUP_SIZE, TT)
    member = lax.broadcasted_iota(I32, (N_GROUPS, GROUP_SIZE, TT), 1)
    group = lax.broadcasted_iota(I32, (N_GROUPS, GROUP_SIZE, TT), 0)
    expert = group * GROUP_SIZE + member
    neg_inf = jnp.float32(-jnp.inf)
    m1 = jnp.max(b3, axis=1, keepdims=True)
    first1 = jnp.min(jnp.where(b3 == m1, member, GROUP_SIZE), axis=1, keepdims=True)
    m2 = jnp.max(jnp.where(member == first1, neg_inf, b3), axis=1, keepdims=True)
    gs = m1 + m2
    gidx = lax.broadcasted_iota(I32, (N_GROUPS, 1, TT), 0)
    grank = jnp.zeros((N_GROUPS, 1, TT), I32)
    for go in range(N_GROUPS):
        other = gs[go:go + 1]
        beats = (other > gs) | ((other == gs) & (go < gidx))
        grank = grank + beats.astype(I32)
    masked = jnp.where(grank < TOPK_GROUPS, b3, neg_inf)
    onehots, gates = [], []
    for _ in range(TOP_K):
        mx = _max01(masked)
        first = _min01(jnp.where(masked == mx, expert, N_EXPERTS))
        hit = expert == first
        onehots.append(hit)
        gates.append(_sum01(jnp.where(hit, s3, 0.0)))
        masked = jnp.where(hit, neg_inf, masked)
    gsum = gates[0]
    for gk in gates[1:]:
        gsum = gsum + gk
    sel3 = jnp.zeros((N_GROUPS, GROUP_SIZE, TT), F32)
    for hit in onehots:
        sel3 = jnp.where(hit, 1.0, sel3)
    sel = sel3.reshape(N_EXPERTS, TT)
    tr = lax.broadcasted_iota(I32, (TT, TT), 0)
    tc = lax.broadcasted_iota(I32, (TT, TT), 1)
    rank = _dot(sel, (tr < tc).astype(F32))
    cnt = jnp.sum(sel, axis=1, keepdims=True)
    pcnt = jnp.floor((cnt + (BF16_ROWS - 1)) * (1.0 / BF16_ROWS)) * BF16_ROWS
    er = lax.broadcasted_iota(I32, (N_EXPERTS, N_EXPERTS), 0)
    ec = lax.broadcasted_iota(I32, (N_EXPERTS, N_EXPERTS), 1)
    pcnt_b = jnp.broadcast_to(pcnt, (N_EXPERTS, LANES))
    loff = _dot_f32((ec < er).astype(F32), pcnt_b)[:, 0:1]
    lpos3 = (loff + rank).reshape(N_GROUPS, GROUP_SIZE, TT)
    for k in range(TOP_K):
        lp = _sum01(jnp.where(onehots[k], lpos3, 0.0))
        lpos_ref[0, k:k + 1, :] = lp.reshape(1, TT).astype(I32)
        gate_ref[0, k:k + 1, :] = (gates[k] / gsum * ROUTED_SCALE).reshape(1, TT)
    pcnt_ref[0] = pcnt_b.astype(I32)


def _post(mix_in, x2, wts, *, interpret=False):
    N = x2.shape[0]
    TT = TOK_TILE
    assert N % TT == 0
    nt = N // TT
    tok = lambda w: pl.BlockSpec((TT, w), lambda i: (i, 0))
    const = lambda a: pl.BlockSpec(a.shape, lambda i, _n=a.ndim: (0,) * _n)
    out_shape = (
        jax.ShapeDtypeStruct((N, D_MODEL), MXU_DTYPE),
        jax.ShapeDtypeStruct((N, D_MODEL), F32),
        jax.ShapeDtypeStruct((nt, TOP_K, TT), I32),
        jax.ShapeDtypeStruct((nt, TOP_K, TT), F32),
        jax.ShapeDtypeStruct((nt, N_EXPERTS, LANES), I32),
    )
    out_specs = (
        tok(D_MODEL), tok(D_MODEL),
        pl.BlockSpec((1, TOP_K, TT), lambda i: (i, 0, 0)),
        pl.BlockSpec((1, TOP_K, TT), lambda i: (i, 0, 0)),
        pl.BlockSpec((1, N_EXPERTS, LANES), lambda i: (i, 0, 0)),
    )
    return pl.pallas_call(
        functools.partial(_post_kernel, tile=TT), grid=(nt,),
        in_specs=[tok(mix_in.shape[1]), tok(D_MODEL)] + [const(w) for w in wts],
        out_specs=out_specs, out_shape=out_shape, name="post",
        compiler_params=pltpu.CompilerParams(
            dimension_semantics=("parallel",), vmem_limit_bytes=VMEM_LIMIT_BYTES),
        interpret=interpret,
    )(mix_in, x2, *wts)


def _post_weights(w_out, ln1_g, ln1_b, router_w, router_bias, sh_w_gate, sh_w_up, sh_w_down):
    return (w_out.astype(MXU_DTYPE), ln1_g.reshape(1, D_MODEL), ln1_b.reshape(1, D_MODEL),
            router_w.T.astype(MXU_DTYPE), router_bias.reshape(N_EXPERTS, 1),
            jnp.concatenate([sh_w_gate, sh_w_up], axis=1).astype(MXU_DTYPE), sh_w_down.astype(MXU_DTYPE))


SORT_CHUNK = 512


def _local_rows(tile):
    worst = tile * TOP_K + N_EXPERTS * (BF16_ROWS - 1)
    return -(-worst // SORT_CHUNK) * SORT_CHUNK


def _run_copy(loc_ref, hbm_ref, sem, loc_row, hbm_row, to_hbm):
    loc = loc_ref.at[pl.ds(pl.multiple_of(loc_row, BF16_ROWS), BF16_ROWS), :]
    hbm = hbm_ref.at[pl.ds(pl.multiple_of(hbm_row, BF16_ROWS), BF16_ROWS), :]
    return pltpu.make_async_copy(loc, hbm, sem) if to_hbm else pltpu.make_async_copy(hbm, loc, sem)


def _move_runs(tab_ref, loc_ref, hbm_ref, sem, to_hbm):
    def per_expert(e, off):
        n = tab_ref[0, 0, e] // BF16_ROWS
        dst0 = tab_ref[0, 0, N_EXPERTS + e]

        def per_granule(j, c):
            _run_copy(loc_ref, hbm_ref, sem, off + j * BF16_ROWS, dst0 + j * BF16_ROWS, to_hbm).start()
            return c

        lax.fori_loop(0, n, per_granule, 0)
        return off + n * BF16_ROWS

    total = lax.fori_loop(0, N_EXPERTS, per_expert, 0)

    def wait_one(j, c):
        _run_copy(loc_ref, hbm_ref, sem, 0, 0, to_hbm).wait()
        return c

    lax.fori_loop(0, total // BF16_ROWS, wait_one, 0)


def _dispatch_kernel(tab_ref, tail_ref, h1b_ref, lpos_ref, xs_ref, xloc, zbuf, sem, *, tile):
    TT = tile
    LR = _local_rows(TT)
    xb = h1b_ref[...]
    lp = lpos_ref[0]
    for rc in range(LR // SORT_CHUNK):
        rows = lax.broadcasted_iota(I32, (SORT_CHUNK, TT), 0) + rc * SORT_CHUNK
        p = jnp.zeros((SORT_CHUNK, TT), F32)
        for k in range(TOP_K):
            p = jnp.where(rows == lp[k:k + 1, :], 1.0, p)
        xl = jnp.dot(p.astype(MXU_DTYPE), xb, preferred_element_type=F32)
        xloc[rc * SORT_CHUNK:(rc + 1) * SORT_CHUNK, :] = xl.astype(xloc.dtype)
    _move_runs(tab_ref, xloc, xs_ref, sem, True)

    @pl.when(pl.program_id(0) == pl.num_programs(0) - 1)
    def _zero_tails():
        zbuf[...] = jnp.zeros_like(zbuf)

        def per_expert(e, tot):
            n = tail_ref[0, 0, N_EXPERTS + e]
            start = tail_ref[0, 0, e]

            def per_granule(j, c):
                _run_copy(zbuf, xs_ref, sem, 0, start + j * BF16_ROWS, True).start()
                return c

            lax.fori_loop(0, n, per_granule, 0)
            return tot + n

        total = lax.fori_loop(0, N_EXPERTS, per_expert, 0)

        def wait_one(j, c):
            _run_copy(zbuf, xs_ref, sem, 0, 0, True).wait()
            return c

        lax.fori_loop(0, total, wait_one, 0)


def _dispatch(h1b, lpos, tab, tail, n_rows, *, interpret=False):
    N = h1b.shape[0]
    TT = TOK_TILE
    nt = N // TT
    return pl.pallas_call(
        functools.partial(_dispatch_kernel, tile=TT), grid=(nt,),
        in_specs=[
            pl.BlockSpec((1, 1, 2 * N_EXPERTS), lambda i: (i, 0, 0), memory_space=pltpu.SMEM),
            pl.BlockSpec((1, 1, 2 * N_EXPERTS), lambda i: (0, 0, 0), memory_space=pltpu.SMEM),
            pl.BlockSpec((TT, D_MODEL), lambda i: (i, 0)),
            pl.BlockSpec((1, TOP_K, TT), lambda i: (i, 0, 0)),
        ],
        out_specs=pl.BlockSpec(memory_space=pl.ANY),
        out_shape=jax.ShapeDtypeStruct((n_rows, D_MODEL), MXU_DTYPE),
        scratch_shapes=[pltpu.VMEM((_local_rows(TT), D_MODEL), MXU_DTYPE),
                        pltpu.VMEM((BF16_ROWS, D_MODEL), MXU_DTYPE),
                        pltpu.SemaphoreType.DMA(())],
        name="dispatch",
        compiler_params=pltpu.CompilerParams(
            dimension_semantics=("arbitrary",), vmem_limit_bytes=VMEM_LIMIT_BYTES),
        interpret=interpret,
    )(tab, tail, h1b, lpos)


def _expert_kernel(blk_e_ref, nact_ref, x_ref, wgu_ref, wd_ref, y_ref):
    j = pl.program_id(0)

    @pl.when(j < nact_ref[0])
    def _compute():
        gu = jnp.dot(x_ref[...], wgu_ref[0], preferred_element_type=F32)
        g = gu[:, :EXPERT_FF]
        h = (g * _sigmoid(g)) * gu[:, EXPERT_FF:]
        y_ref[...] = _dot(h, wd_ref[0]).astype(y_ref.dtype)

    @pl.when(j >= nact_ref[0])
    def _idle():
        y_ref[...] = jnp.zeros_like(y_ref)


def _experts(xs, wgu, wd, blk_e, nact, *, interpret=False):
    BM = ROW_BLOCK
    nblk = xs.shape[0] // BM
    grid_spec = pltpu.PrefetchScalarGridSpec(
        num_scalar_prefetch=2, grid=(nblk,),
        in_specs=[
            pl.BlockSpec((BM, D_MODEL), lambda j, be, na: (jnp.minimum(j, na[0] - 1), 0)),
            pl.BlockSpec((1, D_MODEL, 2 * EXPERT_FF), lambda j, be, na: (be[j], 0, 0)),
            pl.BlockSpec((1, EXPERT_FF, D_MODEL), lambda j, be, na: (be[j], 0, 0)),
        ],
        out_specs=pl.BlockSpec((BM, D_MODEL), lambda j, be, na: (jnp.where(j < na[0], j, nblk), 0)),
    )
    return pl.pallas_call(
        _expert_kernel, grid_spec=grid_spec,
        out_shape=jax.ShapeDtypeStruct(((nblk + 1) * BM, D_MODEL), MXU_DTYPE), name="experts",
        compiler_params=pltpu.CompilerParams(
            dimension_semantics=("arbitrary",), vmem_limit_bytes=VMEM_LIMIT_BYTES),
        interpret=interpret,
    )(blk_e, nact, xs, wgu, wd)


def _combine_kernel(tab_ref, lpos_ref, gate_ref, base_ref, ln2g_ref, ln2b_ref, ys_ref, out_ref, yloc, sem,
                    *, tile):
    TT = tile
    LR = _local_rows(TT)

    @pl.when(pl.program_id(0) == 0)
    def _init():
        yloc[...] = jnp.zeros_like(yloc)

    _move_runs(tab_ref, yloc, ys_ref, sem, False)
    lp = lpos_ref[0]
    gt = gate_ref[0]
    acc = base_ref[...]
    for rc in range(LR // SORT_CHUNK):
        rows = lax.broadcasted_iota(I32, (SORT_CHUNK, TT), 0) + rc * SORT_CHUNK
        w = jnp.zeros((SORT_CHUNK, TT), F32)
        for k in range(TOP_K):
            w = jnp.where(rows == lp[k:k + 1, :], gt[k:k + 1, :], w)
        acc = acc + _dot_tn(w, yloc[rc * SORT_CHUNK:(rc + 1) * SORT_CHUNK, :])
    out_ref[...] = _layer_norm(acc, ln2g_ref[...], ln2b_ref[...])


def _combine(ys, lpos, gate, base, ln2g, ln2b, tab, *, interpret=False):
    N = base.shape[0]
    TT = TOK_TILE
    nt = N // TT
    return pl.pallas_call(
        functools.partial(_combine_kernel, tile=TT), grid=(nt,),
        in_specs=[
            pl.BlockSpec((1, 1, 2 * N_EXPERTS), lambda i: (i, 0, 0), memory_space=pltpu.SMEM),
            pl.BlockSpec((1, TOP_K, TT), lambda i: (i, 0, 0)),
            pl.BlockSpec((1, TOP_K, TT), lambda i: (i, 0, 0)),
            pl.BlockSpec((TT, D_MODEL), lambda i: (i, 0)),
            pl.BlockSpec((1, D_MODEL), lambda i: (0, 0)),
            pl.BlockSpec((1, D_MODEL), lambda i: (0, 0)),
            pl.BlockSpec(memory_space=pl.ANY),
        ],
        out_specs=pl.BlockSpec((TT, D_MODEL), lambda i: (i, 0)),
        out_shape=jax.ShapeDtypeStruct((N, D_MODEL), F32),
        scratch_shapes=[pltpu.VMEM((_local_rows(TT), D_MODEL), MXU_DTYPE), pltpu.SemaphoreType.DMA(())],
        name="combine",
        compiler_params=pltpu.CompilerParams(
            dimension_semantics=("arbitrary",), vmem_limit_bytes=VMEM_LIMIT_BYTES),
        interpret=interpret,
    )(tab, lpos, gate, base, ln2g, ln2b, ys)


def _moe_weights(exp_w_gate, exp_w_up, exp_w_down, ln2_g, ln2_b):
    return (jnp.concatenate([exp_w_gate, exp_w_up], axis=-1).astype(MXU_DTYPE), exp_w_down.astype(MXU_DTYPE),
            ln2_g.reshape(1, D_MODEL), ln2_b.reshape(1, D_MODEL))


def _sorted_layout(pcnt):
    nt = pcnt.shape[0]
    run_len = jnp.sum(pcnt, axis=0)
    region = (run_len + ROW_BLOCK - 1) // ROW_BLOCK * ROW_BLOCK
    region_end = jnp.cumsum(region)
    region_start = region_end - region
    gbase = region_start[None, :] + jnp.cumsum(pcnt, axis=0) - pcnt
    tab = jnp.concatenate([pcnt, gbase], axis=1).reshape(nt, 1, 2 * N_EXPERTS).astype(I32)
    tail = jnp.concatenate([region_start + run_len, (region - run_len) // BF16_ROWS]
                           ).reshape(1, 1, 2 * N_EXPERTS).astype(I32)
    max_rows = nt * (TOK_TILE * TOP_K + N_EXPERTS * (BF16_ROWS - 1)) + N_EXPERTS * (ROW_BLOCK - BF16_ROWS)
    nblk = -(-max_rows // ROW_BLOCK)
    blk_row = jnp.arange(nblk, dtype=I32) * ROW_BLOCK
    blk_e = jnp.minimum(jnp.sum((region_end[None, :] <= blk_row[:, None]).astype(I32), axis=1), N_EXPERTS - 1)
    nact = (region_end[-1] // ROW_BLOCK).reshape(1).astype(I32)
    return tab, tail, blk_e, nact, nblk * ROW_BLOCK


def _layer(x, gbuf0, s0, lbuf0, h0, mix_w, post_w, moe_w, *, chunk, reset_first, interpret=False):
    B, T, D = x.shape
    mix_in, gbuf, s_new, lbuf, h_new = _mixer(x, gbuf0, s0, lbuf0, h0, mix_w, chunk=chunk,
                                              reset_first=reset_first, interpret=interpret)
    N = B * T
    h1b, base, lpos, gate, pcnt = _post(mix_in.reshape(N, -1), x.reshape(N, D), post_w, interpret=interpret)
    tab, tail, blk_e, nact, n_rows = _sorted_layout(pcnt[:, :, 0])
    wgu, wd, ln2g, ln2b = moe_w
    xs = _dispatch(h1b, lpos, tab, tail, n_rows, interpret=interpret)
    ys = _experts(xs, wgu, wd, blk_e, nact, interpret=interpret)
    y = _combine(ys, lpos, gate, base, ln2g, ln2b, tab, interpret=interpret)
    keep = CONV_PAD - (CONV_W - 1)
    states = (gbuf[:, keep:], s_new, lbuf[:, keep:], h_new.reshape(B, D))
    return y.reshape(B, T, D), states


def kernel(x_prompt, x_sample, state_gdn_conv, state_gdn, state_lru_conv, state_lru, w_in, gdn_conv_w, gdn_a_log, gdn_dt_bias, gdn_norm_w, lru_conv_w, lru_conv_b, lru_wa, lru_ba, lru_wx, lru_bx, lru_lambda, w_out, ln1_g, ln1_b, router_w, router_bias, exp_w_gate, exp_w_up, exp_w_down, sh_w_gate, sh_w_up, sh_w_down, ln2_g, ln2_b):
    mix_w = _mixer_weights(w_in[0], gdn_conv_w[0], gdn_a_log[0], gdn_dt_bias[0], gdn_norm_w[0], lru_conv_w[0],
                           lru_conv_b[0], lru_wa[0], lru_ba[0], lru_wx[0], lru_bx[0], lru_lambda[0])
    post_w = _post_weights(w_out[0], ln1_g[0], ln1_b[0], router_w[0], router_bias[0],
                           sh_w_gate[0], sh_w_up[0], sh_w_down[0])
    moe_w = _moe_weights(exp_w_gate[0], exp_w_up[0], exp_w_down[0], ln2_g[0], ln2_b[0])
    bp, bs = x_prompt.shape[0], x_sample.shape[0]
    zeros = lambda *s: jnp.zeros(s, F32)
    pad_rows = lambda a: jnp.pad(a, ((0, 0), (CONV_PAD - (CONV_W - 1), 0), (0, 0)))
    yp, sp = _layer(x_prompt, zeros(bp, CONV_PAD, QKV_GROUPS * LANES), zeros(bp, HEADS, HEAD_DIM, HEAD_DIM),
                    zeros(bp, CONV_PAD, D_MODEL), zeros(bp, LRU_BLOCKS, 1, LANES),
                    mix_w, post_w, moe_w, chunk=64, reset_first=True)
    ys, ss = _layer(x_sample, pad_rows(state_gdn_conv[0]), state_gdn[0], pad_rows(state_lru_conv[0]),
                    state_lru[0].reshape(bs, LRU_BLOCKS, 1, LANES),
                    mix_w, post_w, moe_w, chunk=x_sample.shape[1], reset_first=False)
    return (yp, ys) + tuple(a[None] for a in sp) + tuple(a[None] for a in ss)
```

```python
import functools
import math

import jax
import jax.numpy as jnp
from jax import lax
from jax.experimental import pallas as pl
from jax.experimental.pallas import tpu as pltpu

F32 = jnp.float32
I32 = jnp.int32
MXU_DTYPE = jnp.bfloat16

D_MODEL = 1024
HEADS = 8
HEAD_DIM = 128
QKV_GROUPS = 3 * HEADS
LRU_BLOCKS = 8
CONV_W = 4
CONV_PAD = 8
N_EXPERTS = 64
N_GROUPS = 8
GROUP_SIZE = N_EXPERTS // N_GROUPS
TOPK_GROUPS = 4
TOP_K = 8
EXPERT_FF = 256
SHARED_FF = 256
ROUTED_SCALE = 2.5
LRU_C = 8.0
DEPTH = 1
DN_ALPHA = (2.0 * DEPTH) ** 0.25
LN_EPS = 1e-5
RMS_EPS = 1e-6
L2_EPS = 1e-6

LANES = 128
SUBLANES = 8
BF16_ROWS = 16
VMEM_LIMIT_BYTES = 56 * 1024 * 1024

MIX_TILE = 256
HEAD_GROUP = 4
TOK_TILE = 256
ROW_BLOCK = 512


def _dot(a, b):
    return jnp.dot(a.astype(MXU_DTYPE), b.astype(MXU_DTYPE), preferred_element_type=F32)


def _dot_nt(a, b):
    return lax.dot_general(a.astype(MXU_DTYPE), b.astype(MXU_DTYPE),
                           (((1,), (1,)), ((), ())), preferred_element_type=F32)


def _dot_tn(a, b):
    return lax.dot_general(a.astype(MXU_DTYPE), b.astype(MXU_DTYPE),
                           (((0,), (0,)), ((), ())), preferred_element_type=F32)


def _dot_f32(a, b):
    return jnp.dot(a, b, precision=lax.Precision.HIGHEST, preferred_element_type=F32)


def _sigmoid(x):
    return 1.0 / (1.0 + jnp.exp(-x))


def _softplus(x):
    return jnp.maximum(x, 0.0) + jnp.log1p(jnp.exp(-jnp.abs(x)))


def _widen(col, width):
    if width <= LANES:
        return col[:, :width]
    return jnp.concatenate([col] * (width // LANES), axis=1)


def _mixer_kernel(x_ref, gbuf0_ref, s0_ref, lbuf0_ref, h0_ref,
                  wmain_ref, wba_ref, gcw_ref, alog_ref, dtb_ref, gnw_ref,
                  lcw_ref, lcb_ref, wax_ref, lbax_ref, lam_ref,
                  mix_ref, gbuf_ref, s_ref, lbuf_ref, h_ref,
                  qkv_s, z_s, lx_s, ly_s, col_s, row_s, o_s, hst_s, lsig_s,
                  *, tile, chunk, reset_first):
    TT, C = tile, chunk
    NC = TT // C
    t = pl.program_id(1)

    @pl.when(t == 0)
    def _load_state():
        for g in range(QKV_GROUPS):
            qkv_s[g, 0:CONV_PAD, :] = gbuf0_ref[0, :, g * LANES:(g + 1) * LANES]
        for k in range(LRU_BLOCKS):
            lx_s[k, 0:CONV_PAD, :] = lbuf0_ref[0, :, k * LANES:(k + 1) * LANES]
        s_ref[...] = s0_ref[...]
        hst_s[...] = h0_ref[0]

    xb = x_ref[0].astype(MXU_DTYPE)
    for gp in range(QKV_GROUPS // 2):
        res = jnp.dot(xb, wmain_ref[:, gp * 256:(gp + 1) * 256], preferred_element_type=F32)
        qkv_s[2 * gp, CONV_PAD:CONV_PAD + TT, :] = res[:, :LANES]
        qkv_s[2 * gp + 1, CONV_PAD:CONV_PAD + TT, :] = res[:, LANES:]
    col0 = QKV_GROUPS * LANES
    for gp in range(HEADS // 2):
        res = jnp.dot(xb, wmain_ref[:, col0 + gp * 256:col0 + (gp + 1) * 256], preferred_element_type=F32)
        z_s[2 * gp] = res[:, :LANES]
        z_s[2 * gp + 1] = res[:, LANES:]
    col0 += HEADS * LANES
    for gp in range(LRU_BLOCKS // 2):
        res = jnp.dot(xb, wmain_ref[:, col0 + gp * 256:col0 + (gp + 1) * 256], preferred_element_type=F32)
        lx_s[2 * gp, CONV_PAD:CONV_PAD + TT, :] = res[:, :LANES]
        lx_s[2 * gp + 1, CONV_PAD:CONV_PAD + TT, :] = res[:, LANES:]
    col0 += LRU_BLOCKS * LANES
    for gp in range(LRU_BLOCKS // 2):
        res = jnp.dot(xb, wmain_ref[:, col0 + gp * 256:col0 + (gp + 1) * 256], preferred_element_type=F32)
        ly_s[2 * gp] = res[:, :LANES]
        ly_s[2 * gp + 1] = res[:, LANES:]
    ba = jnp.dot(xb, wba_ref[...], preferred_element_type=F32)

    for g in range(QKV_GROUPS):
        gbuf_ref[0, :, g * LANES:(g + 1) * LANES] = qkv_s[g, TT:TT + CONV_PAD, :]
    for k in range(LRU_BLOCKS):
        lbuf_ref[0, :, k * LANES:(k + 1) * LANES] = lx_s[k, TT:TT + CONV_PAD, :]

    row = lax.broadcasted_iota(I32, (TT, TT), 0)
    colm = lax.broadcasted_iota(I32, (TT, TT), 1)
    same_chunk = (row // C) == (colm // C)
    causal = same_chunk & (row >= colm)
    strict = same_chunk & (row > colm)
    beta_all = _sigmoid(ba)
    g_all = -jnp.exp(alog_ref[...]) * _softplus(ba + dtb_ref[...])
    gc_all = _dot_f32(causal.astype(F32), g_all)
    gt_all = _dot_f32(same_chunk.astype(F32), g_all)
    gc_t = gc_all.T
    for h in range(HEADS):
        col_s[0, h] = jnp.broadcast_to(beta_all[:, h:h + 1], (TT, LANES))
        col_s[1, h] = jnp.broadcast_to(gc_all[:, HEADS + h:HEADS + h + 1], (TT, LANES))
        col_s[2, h] = jnp.broadcast_to(gt_all[:, HEADS + h:HEADS + h + 1], (TT, LANES))
        row_s[h] = gc_t[HEADS + h:HEADS + h + 1, :]

    def conv_silu(g):
        w = gcw_ref[g]
        acc = qkv_s[g, pl.ds(CONV_PAD - 3, TT), :] * w[0:1]
        acc = acc + qkv_s[g, pl.ds(CONV_PAD - 2, TT), :] * w[1:2]
        acc = acc + qkv_s[g, pl.ds(CONV_PAD - 1, TT), :] * w[2:3]
        acc = acc + qkv_s[g, pl.ds(CONV_PAD, TT), :] * w[3:4]
        return acc * _sigmoid(acc)

    n_levels = int(math.log2(C))
    cat = lambda parts, axis: parts[0] if len(parts) == 1 else jnp.concatenate(parts, axis=axis)
    G = TT // SUBLANES
    sub = lax.broadcasted_iota(I32, (G, SUBLANES, LANES), 1)
    first_row = lax.broadcasted_iota(I32, (TT, LANES), 0) == 0

    def lru_block(kb_):
        w = lcw_ref[kb_]
        xc = lx_s[kb_, pl.ds(CONV_PAD - 3, TT), :] * w[0:1]
        xc = xc + lx_s[kb_, pl.ds(CONV_PAD - 2, TT), :] * w[1:2]
        xc = xc + lx_s[kb_, pl.ds(CONV_PAD - 1, TT), :] * w[2:3]
        xc = xc + lx_s[kb_, pl.ds(CONV_PAD, TT), :] * w[3:4]
        xc = xc + lcb_ref[kb_]
        gates = _sigmoid(_dot(xc, wax_ref[kb_]) + lbax_ref[kb_])
        r = gates[:, :LANES]
        gi = gates[:, LANES:]
        log_a = (LRU_C * r) * lsig_s[kb_]
        a = jnp.exp(log_a)
        th = jnp.tanh(log_a)
        mult = jnp.sqrt(-2.0 * th / (1.0 - th))
        if reset_first:
            mult = jnp.where(first_row & (t == 0), 1.0, mult)
        hprev = hst_s[kb_]
        b = mult * gi * xc + jnp.where(first_row, a * hprev, 0.0)
        a3 = a.reshape(G, SUBLANES, LANES)
        b3 = b.reshape(G, SUBLANES, LANES)
        for s in (1, 2, 4):
            a_sh = jnp.where(sub >= s, pltpu.roll(a3, s, 1), 1.0)
            b_sh = jnp.where(sub >= s, pltpu.roll(b3, s, 1), 0.0)
            b3 = a3 * b_sh + b3
            a3 = a3 * a_sh
        rows = []
        hc = jnp.zeros((1, LANES), F32)
        for gidx in range(G):
            hgrp = b3[gidx] if gidx == 0 else a3[gidx] * hc + b3[gidx]
            hc = hgrp[SUBLANES - 1:SUBLANES, :]
            rows.append(hgrp)
        hst_s[kb_] = hc
        o_s[HEADS + kb_] = jnp.concatenate(rows, axis=0) * jax.nn.gelu(ly_s[kb_])

    lsig_s[...] = -_softplus(-lam_ref[...])
    n_groups = HEADS // HEAD_GROUP
    lru_per_group = LRU_BLOCKS // n_groups

    def head_group_body(hg, carry):
        hs = [hg * HEAD_GROUP + i for i in range(HEAD_GROUP)]
        R = range(HEAD_GROUP)
        lru_todo = [hg * lru_per_group + i for i in range(lru_per_group)]

        def lru_step():
            if lru_todo:
                lru_block(lru_todo.pop(0))

        q = [conv_silu(h) for h in hs]
        k = [conv_silu(HEADS + h) for h in hs]
        v = [conv_silu(2 * HEADS + h) for h in hs]
        q = [x * lax.rsqrt(jnp.sum(x * x, axis=-1, keepdims=True) + L2_EPS) * (HEAD_DIM ** -0.5) for x in q]
        k = [x * lax.rsqrt(jnp.sum(x * x, axis=-1, keepdims=True) + L2_EPS) for x in k]
        beta = [col_s[0, h] for h in hs]
        gcol = [col_s[1, h] for h in hs]
        gtot = [col_s[2, h] for h in hs]
        decay = [jnp.where(causal, jnp.exp(_widen(gcol[i], TT) - row_s[hs[i]]), 0.0) for i in R]
        eg = [jnp.exp(g) for g in gcol]
        kb = [k[i] * beta[i] for i in R]
        qks = [_dot_nt(jnp.concatenate([kb[i], q[i]], axis=0), k[i]) for i in R]
        nmat = [jnp.where(strict, -(qks[i][:TT] * decay[i]), 0.0) for i in R]
        qk = [qks[i][TT:] * decay[i] for i in R]
        lru_step()
        tm = nmat
        if n_levels > 1:
            npow = [_dot(n, n) for n in nmat]
            for j in range(1, n_levels):
                if j < n_levels - 1:
                    r2 = [_dot(jnp.concatenate([tm[i], npow[i]], axis=0), npow[i]) for i in R]
                    tm = [tm[i] + npow[i] + r2[i][:TT] for i in R]
                    npow = [r2[i][TT:] for i in R]
                else:
                    tm = [tm[i] + npow[i] + _dot(tm[i], npow[i]) for i in R]
                lru_step()
        rhs =[jnp.concatenate([kb[i] * eg[i], v[i] * beta[i]], axis=1) for i in R]
        wu = [rhs[i] + _dot(tm[i], rhs[i]) for i in R]
        qd = [q[i] * eg[i] for i in R]
        kd = [k[i] * jnp.exp(gtot[i] - gcol[i]) for i in R]
        egt = [jnp.exp(g) for g in gtot]
        kwu = [[_dot_tn(kd[i][c * C:(c + 1) * C], wu[i][c * C:(c + 1) * C]) for c in range(NC)] for i in R]
        state = [s_ref[0, h] for h in hs]
        starts = [[] for _ in R]
        for c in range(NC):
            for i in R:
                starts[i].append(state[i])
                m = kwu[i][c]
                state[i] = (state[i] * egt[i][c * C:c * C + 1, :] + m[:, LANES:]) - _dot(m[:, :LANES], state[i])
        for i in R:
            s_ref[0, hs[i]] = state[i]
        while lru_todo:
            lru_step()
        for i in R:
            wq =[_dot(jnp.concatenate([wu[i][c * C:(c + 1) * C, :LANES], qd[i][c * C:(c + 1) * C]], axis=0),
                       starts[i][c]) for c in range(NC)]
            vn = cat([wu[i][c * C:(c + 1) * C, LANES:] - wq[c][:C] for c in range(NC)], 0)
            o = cat([wq[c][C:] for c in range(NC)], 0) + _dot(qk[i], vn)
            o = o * lax.rsqrt(jnp.mean(o * o, axis=-1, keepdims=True) + RMS_EPS) * gnw_ref[...]
            zz = z_s[hs[i]]
            o_s[hs[i]] = o * (zz * _sigmoid(zz))
        return carry

    lax.fori_loop(0, HEADS // HEAD_GROUP, head_group_body, 0)

    for g in range(HEADS + LRU_BLOCKS):
        mix_ref[0, :, g * LANES:(g + 1) * LANES] = o_s[g].astype(mix_ref.dtype)
    h_ref[0] = hst_s[...]
    for g in range(QKV_GROUPS):
        qkv_s[g, 0:CONV_PAD, :] = qkv_s[g, TT:TT + CONV_PAD, :]
    for k in range(LRU_BLOCKS):
        lx_s[k, 0:CONV_PAD, :] = lx_s[k, TT:TT + CONV_PAD, :]


def _const_spec(shape):
    nd = len(shape)
    return pl.BlockSpec(shape, lambda b, t, _n=nd: (0,) * _n)


def _mixer(x, gbuf0, s0, lbuf0, h0, wts, *, chunk, reset_first, interpret=False):
    B, T, _ = x.shape
    TT = min(MIX_TILE, T)
    assert T % TT == 0 and TT % chunk == 0 and TT % SUBLANES == 0
    NT = T // TT
    kern = functools.partial(_mixer_kernel, tile=TT, chunk=chunk, reset_first=reset_first)
    in_specs = [
        pl.BlockSpec((1, TT, D_MODEL), lambda b, t: (b, t, 0)),
        pl.BlockSpec((1, CONV_PAD, QKV_GROUPS * LANES), lambda b, t: (b, 0, 0)),
        pl.BlockSpec((1, HEADS, HEAD_DIM, HEAD_DIM), lambda b, t: (b, 0, 0, 0)),
        pl.BlockSpec((1, CONV_PAD, LRU_BLOCKS * LANES), lambda b, t: (b, 0, 0)),
        pl.BlockSpec((1, LRU_BLOCKS, 1, LANES), lambda b, t: (b, 0, 0, 0)),
    ] + [_const_spec(w.shape) for w in wts]
    out_shape = (
        jax.ShapeDtypeStruct((B, T, (HEADS + LRU_BLOCKS) * LANES), MXU_DTYPE),
        jax.ShapeDtypeStruct((B, CONV_PAD, QKV_GROUPS * LANES), F32),
        jax.ShapeDtypeStruct((B, HEADS, HEAD_DIM, HEAD_DIM), F32),
        jax.ShapeDtypeStruct((B, CONV_PAD, LRU_BLOCKS * LANES), F32),
        jax.ShapeDtypeStruct((B, LRU_BLOCKS, 1, LANES), F32),
    )
    out_specs = (
        pl.BlockSpec((1, TT, (HEADS + LRU_BLOCKS) * LANES), lambda b, t: (b, t, 0)),
        pl.BlockSpec((1, CONV_PAD, QKV_GROUPS * LANES), lambda b, t: (b, 0, 0)),
        pl.BlockSpec((1, HEADS, HEAD_DIM, HEAD_DIM), lambda b, t: (b, 0, 0, 0)),
        pl.BlockSpec((1, CONV_PAD, LRU_BLOCKS * LANES), lambda b, t: (b, 0, 0)),
        pl.BlockSpec((1, LRU_BLOCKS, 1, LANES), lambda b, t: (b, 0, 0, 0)),
    )
    scratch = [
        pltpu.VMEM((QKV_GROUPS, TT + CONV_PAD, LANES), F32),
        pltpu.VMEM((HEADS, TT, LANES), F32),
        pltpu.VMEM((LRU_BLOCKS, TT + CONV_PAD, LANES), F32),
        pltpu.VMEM((LRU_BLOCKS, TT, LANES), F32),
        pltpu.VMEM((3, HEADS, TT, LANES), F32),
        pltpu.VMEM((HEADS, 1, TT), F32),
        pltpu.VMEM((HEADS + LRU_BLOCKS, TT, LANES), F32),
        pltpu.VMEM((LRU_BLOCKS, 1, LANES), F32),
        pltpu.VMEM((LRU_BLOCKS, 1, LANES), F32),
    ]
    return pl.pallas_call(
        kern, grid=(B, NT), in_specs=in_specs, out_specs=out_specs, out_shape=out_shape,
        scratch_shapes=scratch, name="mixer",
        compiler_params=pltpu.CompilerParams(
            dimension_semantics=("parallel", "arbitrary"), vmem_limit_bytes=VMEM_LIMIT_BYTES),
        interpret=interpret,
    )(x, gbuf0, s0, lbuf0, h0, *wts)


def _mixer_weights(w_in, gdn_conv_w, gdn_a_log, gdn_dt_bias, gdn_norm_w,
                   lru_conv_w, lru_conv_b, lru_wa, lru_ba, lru_wx, lru_bx, lru_lambda):
    qkvz = QKV_GROUPS * LANES + HEADS * LANES
    ba0 = qkvz
    lx0 = qkvz + 2 * HEADS
    wmain = jnp.concatenate([w_in[:, :qkvz], w_in[:, lx0:]], axis=1).astype(MXU_DTYPE)
    wba = jnp.pad(w_in[:, ba0:lx0], ((0, 0), (0, LANES - 2 * HEADS))).astype(MXU_DTYPE)
    gcw = gdn_conv_w.reshape(CONV_W, QKV_GROUPS, LANES).transpose(1, 0, 2)
    pad8 = (HEADS, LANES - 2 * HEADS)
    alog = jnp.pad(gdn_a_log, pad8).reshape(1, LANES)
    dtb = jnp.pad(gdn_dt_bias, pad8).reshape(1, LANES)
    gnw = gdn_norm_w.reshape(1, LANES)
    lcw = lru_conv_w.reshape(CONV_W, LRU_BLOCKS, LANES).transpose(1, 0, 2)
    lcb = lru_conv_b.reshape(LRU_BLOCKS, 1, LANES)
    wax = jnp.concatenate([lru_wa, lru_wx], axis=-1).astype(MXU_DTYPE)
    lbax = jnp.concatenate([lru_ba.reshape(LRU_BLOCKS, 1, LANES),
                            lru_bx.reshape(LRU_BLOCKS, 1, LANES)], axis=-1)
    lam = lru_lambda.reshape(LRU_BLOCKS, 1, LANES)
    return (wmain, wba, gcw, alog, dtb, gnw, lcw, lcb, wax, lbax, lam)


def _layer_norm(x, g, b):
    mu = jnp.mean(x, axis=-1, keepdims=True)
    xc = x - mu
    var = jnp.mean(xc * xc, axis=-1, keepdims=True)
    return xc * lax.rsqrt(var + LN_EPS) * g + b


def _max01(x):
    return jnp.max(jnp.max(x, axis=0, keepdims=True), axis=1, keepdims=True)


def _min01(x):
    return jnp.min(jnp.min(x, axis=0, keepdims=True), axis=1, keepdims=True)


def _sum01(x):
    return jnp.sum(jnp.sum(x, axis=0, keepdims=True), axis=1, keepdims=True)


def _post_kernel(mix_ref, x_ref, wout_ref, ln1g_ref, ln1b_ref, rwt_ref, rbias_ref, shgu_ref, shd_ref,
                 h1b_ref, base_ref, lpos_ref, gate_ref, pcnt_ref, *, tile):
    TT = tile
    hpre = DN_ALPHA * x_ref[...] + jnp.dot(mix_ref[...], wout_ref[...], preferred_element_type=F32)
    h1 = _layer_norm(hpre, ln1g_ref[...], ln1b_ref[...])
    h1b = h1.astype(MXU_DTYPE)
    h1b_ref[...] = h1b

    gu = jnp.dot(h1b, shgu_ref[...], preferred_element_type=F32)
    g_sh = gu[:, :SHARED_FF]
    hsh = (g_sh * _sigmoid(g_sh)) * gu[:, SHARED_FF:]
    base_ref[...] = DN_ALPHA * h1 + _dot(hsh, shd_ref[...])

    logits = lax.dot_general(rwt_ref[...], h1b, (((1,), (1,)), ((), ())), preferred_element_type=F32)
    scores = _sigmoid(logits)
    s3 = scores.reshape(N_GROUPS, GROUP_SIZE, TT)
    b3 = (scores + rbias_ref[...]).reshape(N_GROUPS, GROUP_SIZE, TT)
    member = lax.broadcasted_iota(I32, (N_GROUPS, GROUP_SIZE, TT), 1)
    group = lax.broadcasted_iota(I32, (N_GROUPS, GROUP_SIZE, TT), 0)
    expert = group * GROUP_SIZE + member
    neg_inf = jnp.float32(-jnp.inf)
    m1 = jnp.max(b3, axis=1, keepdims=True)
    first1 = jnp.min(jnp.where(b3 == m1, member, GROUP_SIZE), axis=1, keepdims=True)
    m2 = jnp.max(jnp.where(member == first1, neg_inf, b3), axis=1, keepdims=True)
    gs = m1 + m2
    gidx = lax.broadcasted_iota(I32, (N_GROUPS, 1, TT), 0)
    grank = jnp.zeros((N_GROUPS, 1, TT), I32)
    for go in range(N_GROUPS):
        other = gs[go:go + 1]
        beats = (other > gs) | ((other == gs) & (go < gidx))
        grank = grank + beats.astype(I32)
    masked = jnp.where(grank < TOPK_GROUPS, b3, neg_inf)
    onehots, gates = [], []
    for _ in range(TOP_K):
        mx = _max01(masked)
        first = _min01(jnp.where(masked == mx, expert, N_EXPERTS))
        hit = expert == first
        onehots.append(hit)
        gates.append(_sum01(jnp.where(hit, s3, 0.0)))
        masked = jnp.where(hit, neg_inf, masked)
    gsum = gates[0]
    for gk in gates[1:]:
        gsum = gsum + gk
    sel3 = jnp.zeros((N_GROUPS, GROUP_SIZE, TT), F32)
    for hit in onehots:
        sel3 = jnp.where(hit, 1.0, sel3)
    sel = sel3.reshape(N_EXPERTS, TT)
    tr = lax.broadcasted_iota(I32, (TT, TT), 0)
    tc = lax.broadcasted_iota(I32, (TT, TT), 1)
    rank = _dot(sel, (tr < tc).astype(F32))
    cnt = jnp.sum(sel, axis=1, keepdims=True)
    pcnt = jnp.floor((cnt + (BF16_ROWS - 1)) * (1.0 / BF16_ROWS)) * BF16_ROWS
    er = lax.broadcasted_iota(I32, (N_EXPERTS, N_EXPERTS), 0)
    ec = lax.broadcasted_iota(I32, (N_EXPERTS, N_EXPERTS), 1)
    pcnt_b = jnp.broadcast_to(pcnt, (N_EXPERTS, LANES))
    loff = _dot_f32((ec < er).astype(F32), pcnt_b)[:, 0:1]
    lpos3 = (loff + rank).reshape(N_GROUPS, GROUP_SIZE, TT)
    for k in range(TOP_K):
        lp = _sum01(jnp.where(onehots[k], lpos3, 0.0))
        lpos_ref[0, k:k + 1, :] = lp.reshape(1, TT).astype(I32)
        gate_ref[0, k:k + 1, :] = (gates[k] / gsum * ROUTED_SCALE).reshape(1, TT)
    pcnt_ref[0] = pcnt_b.astype(I32)


def _post(mix_in, x2, wts, *, interpret=False):
    N = x2.shape[0]
    TT = TOK_TILE
    assert N % TT == 0
    nt = N // TT
    tok = lambda w: pl.BlockSpec((TT, w), lambda i: (i, 0))
    const = lambda a: pl.BlockSpec(a.shape, lambda i, _n=a.ndim: (0,) * _n)
    out_shape = (
        jax.ShapeDtypeStruct((N, D_MODEL), MXU_DTYPE),
        jax.ShapeDtypeStruct((N, D_MODEL), F32),
        jax.ShapeDtypeStruct((nt, TOP_K, TT), I32),
        jax.ShapeDtypeStruct((nt, TOP_K, TT), F32),
        jax.ShapeDtypeStruct((nt, N_EXPERTS, LANES), I32),
    )
    out_specs = (
        tok(D_MODEL), tok(D_MODEL),
        pl.BlockSpec((1, TOP_K, TT), lambda i: (i, 0, 0)),
        pl.BlockSpec((1, TOP_K, TT), lambda i: (i, 0, 0)),
        pl.BlockSpec((1, N_EXPERTS, LANES), lambda i: (i, 0, 0)),
    )
    return pl.pallas_call(
        functools.partial(_post_kernel, tile=TT), grid=(nt,),
        in_specs=[tok(mix_in.shape[1]), tok(D_MODEL)] + [const(w) for w in wts],
        out_specs=out_specs, out_shape=out_shape, name="post",
        compiler_params=pltpu.CompilerParams(
            dimension_semantics=("parallel",), vmem_limit_bytes=VMEM_LIMIT_BYTES),
        interpret=interpret,
    )(mix_in, x2, *wts)


def _post_weights(w_out, ln1_g, ln1_b, router_w, router_bias, sh_w_gate, sh_w_up, sh_w_down):
    return (w_out.astype(MXU_DTYPE), ln1_g.reshape(1, D_MODEL), ln1_b.reshape(1, D_MODEL),
            router_w.T.astype(MXU_DTYPE), router_bias.reshape(N_EXPERTS, 1),
            jnp.concatenate([sh_w_gate, sh_w_up], axis=1).astype(MXU_DTYPE), sh_w_down.astype(MXU_DTYPE))


SORT_CHUNK = 512


def _local_rows(tile):
    worst = tile * TOP_K + N_EXPERTS * (BF16_ROWS - 1)
    return -(-worst // SORT_CHUNK) * SORT_CHUNK


def _run_copy(loc_ref, hbm_ref, sem, loc_row, hbm_row, to_hbm):
    loc = loc_ref.at[pl.ds(pl.multiple_of(loc_row, BF16_ROWS), BF16_ROWS), :]
    hbm = hbm_ref.at[pl.ds(pl.multiple_of(hbm_row, BF16_ROWS), BF16_ROWS), :]
    return pltpu.make_async_copy(loc, hbm, sem) if to_hbm else pltpu.make_async_copy(hbm, loc, sem)


def _move_runs(tab_ref, loc_ref, hbm_ref, sem, to_hbm):
    def per_expert(e, off):
        n = tab_ref[0, 0, e] // BF16_ROWS
        dst0 = tab_ref[0, 0, N_EXPERTS + e]

        def per_granule(j, c):
            _run_copy(loc_ref, hbm_ref, sem, off + j * BF16_ROWS, dst0 + j * BF16_ROWS, to_hbm).start()
            return c

        lax.fori_loop(0, n, per_granule, 0)
        return off + n * BF16_ROWS

    total = lax.fori_loop(0, N_EXPERTS, per_expert, 0)

    def wait_one(j, c):
        _run_copy(loc_ref, hbm_ref, sem, 0, 0, to_hbm).wait()
        return c

    lax.fori_loop(0, total // BF16_ROWS, wait_one, 0)


def _tile_rows(tab_ref):
    return lax.fori_loop(0, N_EXPERTS, lambda e, tot: tot + tab_ref[0, 0, e], 0)


def _dispatch_kernel(tab_ref, tail_ref, h1b_ref, lpos_ref, xs_ref, xloc, zbuf, sem, *, tile):
    TT = tile
    LR = _local_rows(TT)
    used_rows = _tile_rows(tab_ref)
    for rc in range(LR // SORT_CHUNK):
        @pl.when(rc * SORT_CHUNK < used_rows)
        def _sort_chunk():
            lp = lpos_ref[0]
            rows = lax.broadcasted_iota(I32, (SORT_CHUNK, TT), 0) + rc * SORT_CHUNK
            p = jnp.zeros((SORT_CHUNK, TT), F32)
            for k in range(TOP_K):
                p = jnp.where(rows == lp[k:k + 1, :], 1.0, p)
            xl = jnp.dot(p.astype(MXU_DTYPE), h1b_ref[...], preferred_element_type=F32)
            xloc[rc * SORT_CHUNK:(rc + 1) * SORT_CHUNK, :] = xl.astype(xloc.dtype)
    _move_runs(tab_ref, xloc, xs_ref, sem, True)

    @pl.when(pl.program_id(0) == pl.num_programs(0) - 1)
    def _zero_tails():
        zbuf[...] = jnp.zeros_like(zbuf)

        def per_expert(e, tot):
            n = tail_ref[0, 0, N_EXPERTS + e]
            start = tail_ref[0, 0, e]

            def per_granule(j, c):
                _run_copy(zbuf, xs_ref, sem, 0, start + j * BF16_ROWS, True).start()
                return c

            lax.fori_loop(0, n, per_granule, 0)
            return tot + n

        total = lax.fori_loop(0, N_EXPERTS, per_expert, 0)

        def wait_one(j, c):
            _run_copy(zbuf, xs_ref, sem, 0, 0, True).wait()
            return c

        lax.fori_loop(0, total, wait_one, 0)


def _dispatch(h1b, lpos, tab, tail, n_rows, *, interpret=False):
    N = h1b.shape[0]
    TT = TOK_TILE
    nt = N // TT
    return pl.pallas_call(
        functools.partial(_dispatch_kernel, tile=TT), grid=(nt,),
        in_specs=[
            pl.BlockSpec((1, 1, 2 * N_EXPERTS), lambda i: (i, 0, 0), memory_space=pltpu.SMEM),
            pl.BlockSpec((1, 1, 2 * N_EXPERTS), lambda i: (0, 0, 0), memory_space=pltpu.SMEM),
            pl.BlockSpec((TT, D_MODEL), lambda i: (i, 0)),
            pl.BlockSpec((1, TOP_K, TT), lambda i: (i, 0, 0)),
        ],
        out_specs=pl.BlockSpec(memory_space=pl.ANY),
        out_shape=jax.ShapeDtypeStruct((n_rows, D_MODEL), MXU_DTYPE),
        scratch_shapes=[pltpu.VMEM((_local_rows(TT), D_MODEL), MXU_DTYPE),
                        pltpu.VMEM((BF16_ROWS, D_MODEL), MXU_DTYPE),
                        pltpu.SemaphoreType.DMA(())],
        name="dispatch",
        compiler_params=pltpu.CompilerParams(
            dimension_semantics=("arbitrary",), vmem_limit_bytes=VMEM_LIMIT_BYTES),
        interpret=interpret,
    )(tab, tail, h1b, lpos)


def _expert_kernel(blk_e_ref, nact_ref, x_ref, wgu_ref, wd_ref, y_ref):
    j = pl.program_id(0)

    @pl.when(j < nact_ref[0])
    def _compute():
        half = x_ref.shape[0] // 2
        parts = [slice(0, half), slice(half, 2 * half)]
        gu = [jnp.dot(x_ref[p, :], wgu_ref[0], preferred_element_type=F32) for p in parts]
        h = [(a[:, :EXPERT_FF] * _sigmoid(a[:, :EXPERT_FF])) * a[:, EXPERT_FF:] for a in gu]
        for p, hp in zip(parts, h):
            y_ref[p, :] = _dot(hp, wd_ref[0]).astype(y_ref.dtype)

    @pl.when(j >= nact_ref[0])
    def _idle():
        y_ref[...] = jnp.zeros_like(y_ref)


def _experts(xs, wgu, wd, blk_e, nact, row_block, *, interpret=False):
    BM = row_block
    nblk = xs.shape[0] // BM
    grid_spec = pltpu.PrefetchScalarGridSpec(
        num_scalar_prefetch=2, grid=(nblk,),
        in_specs=[
            pl.BlockSpec((BM, D_MODEL), lambda j, be, na: (jnp.minimum(j, na[0] - 1), 0)),
            pl.BlockSpec((1, D_MODEL, 2 * EXPERT_FF), lambda j, be, na: (be[j], 0, 0)),
            pl.BlockSpec((1, EXPERT_FF, D_MODEL), lambda j, be, na: (be[j], 0, 0)),
        ],
        out_specs=pl.BlockSpec((BM, D_MODEL), lambda j, be, na: (jnp.where(j < na[0], j, nblk), 0)),
    )
    return pl.pallas_call(
        _expert_kernel, grid_spec=grid_spec,
        out_shape=jax.ShapeDtypeStruct(((nblk + 1) * BM, D_MODEL), MXU_DTYPE), name="experts",
        compiler_params=pltpu.CompilerParams(
            dimension_semantics=("arbitrary",), vmem_limit_bytes=VMEM_LIMIT_BYTES),
        interpret=interpret,
    )(blk_e, nact, xs, wgu, wd)


def _combine_kernel(tab_ref, lpos_ref, gate_ref, base_ref, ln2g_ref, ln2b_ref, ys_ref, out_ref, yloc, sem,
                    *, tile):
    TT = tile
    LR = _local_rows(TT)

    @pl.when(pl.program_id(0) == 0)
    def _init():
        yloc[...] = jnp.zeros_like(yloc)

    _move_runs(tab_ref, yloc, ys_ref, sem, False)
    used_rows = _tile_rows(tab_ref)
    out_ref[...] = base_ref[...]
    for rc in range(LR // SORT_CHUNK):
        @pl.when(rc * SORT_CHUNK < used_rows)
        def _unsort_chunk():
            lp = lpos_ref[0]
            gt = gate_ref[0]
            rows = lax.broadcasted_iota(I32, (SORT_CHUNK, TT), 0) + rc * SORT_CHUNK
            w = jnp.zeros((SORT_CHUNK, TT), F32)
            for k in range(TOP_K):
                w = jnp.where(rows == lp[k:k + 1, :], gt[k:k + 1, :], w)
            out_ref[...] += _dot_tn(w, yloc[rc * SORT_CHUNK:(rc + 1) * SORT_CHUNK, :])
    out_ref[...] = _layer_norm(out_ref[...], ln2g_ref[...], ln2b_ref[...])


def _combine(ys, lpos, gate, base, ln2g, ln2b, tab, *, interpret=False):
    N = base.shape[0]
    TT = TOK_TILE
    nt = N // TT
    return pl.pallas_call(
        functools.partial(_combine_kernel, tile=TT), grid=(nt,),
        in_specs=[
            pl.BlockSpec((1, 1, 2 * N_EXPERTS), lambda i: (i, 0, 0), memory_space=pltpu.SMEM),
            pl.BlockSpec((1, TOP_K, TT), lambda i: (i, 0, 0)),
            pl.BlockSpec((1, TOP_K, TT), lambda i: (i, 0, 0)),
            pl.BlockSpec((TT, D_MODEL), lambda i: (i, 0)),
            pl.BlockSpec((1, D_MODEL), lambda i: (0, 0)),
            pl.BlockSpec((1, D_MODEL), lambda i: (0, 0)),
            pl.BlockSpec(memory_space=pl.ANY),
        ],
        out_specs=pl.BlockSpec((TT, D_MODEL), lambda i: (i, 0)),
        out_shape=jax.ShapeDtypeStruct((N, D_MODEL), F32),
        scratch_shapes=[pltpu.VMEM((_local_rows(TT), D_MODEL), MXU_DTYPE), pltpu.SemaphoreType.DMA(())],
        name="combine",
        compiler_params=pltpu.CompilerParams(
            dimension_semantics=("arbitrary",), vmem_limit_bytes=VMEM_LIMIT_BYTES),
        interpret=interpret,
    )(tab, lpos, gate, base, ln2g, ln2b, ys)


def _moe_weights(exp_w_gate, exp_w_up, exp_w_down, ln2_g, ln2_b):
    return (jnp.concatenate([exp_w_gate, exp_w_up], axis=-1).astype(MXU_DTYPE), exp_w_down.astype(MXU_DTYPE),
            ln2_g.reshape(1, D_MODEL), ln2_b.reshape(1, D_MODEL))


def _row_block(n_tokens):
    rb = ROW_BLOCK
    while rb > LANES and n_tokens * TOP_K // N_EXPERTS < rb:
        rb //= 2
    return rb


def _sorted_layout(pcnt, rb):
    nt = pcnt.shape[0]
    run_len = jnp.sum(pcnt, axis=0)
    region = (run_len + rb - 1) // rb * rb
    region_end = jnp.cumsum(region)
    region_start = region_end - region
    gbase = region_start[None, :] + jnp.cumsum(pcnt, axis=0) - pcnt
    tab = jnp.concatenate([pcnt, gbase], axis=1).reshape(nt, 1, 2 * N_EXPERTS).astype(I32)
    tail = jnp.concatenate([region_start + run_len, (region - run_len) // BF16_ROWS]
                           ).reshape(1, 1, 2 * N_EXPERTS).astype(I32)
    max_rows = nt * (TOK_TILE * TOP_K + N_EXPERTS * (BF16_ROWS - 1)) + N_EXPERTS * (rb - BF16_ROWS)
    nblk = -(-max_rows // rb)
    blk_row = jnp.arange(nblk, dtype=I32) * rb
    blk_e = jnp.minimum(jnp.sum((region_end[None, :] <= blk_row[:, None]).astype(I32), axis=1), N_EXPERTS - 1)
    nact = (region_end[-1] // rb).reshape(1).astype(I32)
    return tab, tail, blk_e, nact, nblk * rb


def _layer(x, gbuf0, s0, lbuf0, h0, mix_w, post_w, moe_w, *, chunk, reset_first, interpret=False):
    B, T, D = x.shape
    mix_in, gbuf, s_new, lbuf, h_new = _mixer(x, gbuf0, s0, lbuf0, h0, mix_w, chunk=chunk,
                                              reset_first=reset_first, interpret=interpret)
    N = B * T
    h1b, base, lpos, gate, pcnt = _post(mix_in.reshape(N, -1), x.reshape(N, D), post_w, interpret=interpret)
    rb = _row_block(N)
    tab, tail, blk_e, nact, n_rows = _sorted_layout(pcnt[:, :, 0], rb)
    wgu, wd, ln2g, ln2b = moe_w
    xs = _dispatch(h1b, lpos, tab, tail, n_rows, interpret=interpret)
    ys = _experts(xs, wgu, wd, blk_e, nact, rb, interpret=interpret)
    y = _combine(ys, lpos, gate, base, ln2g, ln2b, tab, interpret=interpret)
    keep = CONV_PAD - (CONV_W - 1)
    states = (gbuf[:, keep:], s_new, lbuf[:, keep:], h_new.reshape(B, D))
    return y.reshape(B, T, D), states


def kernel(x_prompt, x_sample, state_gdn_conv, state_gdn, state_lru_conv, state_lru, w_in, gdn_conv_w, gdn_a_log, gdn_dt_bias, gdn_norm_w, lru_conv_w, lru_conv_b, lru_wa, lru_ba, lru_wx, lru_bx, lru_lambda, w_out, ln1_g, ln1_b, router_w, router_bias, exp_w_gate, exp_w_up, exp_w_down, sh_w_gate, sh_w_up, sh_w_down, ln2_g, ln2_b):
    mix_w = _mixer_weights(w_in[0], gdn_conv_w[0], gdn_a_log[0], gdn_dt_bias[0], gdn_norm_w[0], lru_conv_w[0],
                           lru_conv_b[0], lru_wa[0], lru_ba[0], lru_wx[0], lru_bx[0], lru_lambda[0])
    post_w = _post_weights(w_out[0], ln1_g[0], ln1_b[0], router_w[0], router_bias[0],
                           sh_w_gate[0], sh_w_up[0], sh_w_down[0])
    moe_w = _moe_weights(exp_w_gate[0], exp_w_up[0], exp_w_down[0], ln2_g[0], ln2_b[0])
    bp, bs = x_prompt.shape[0], x_sample.shape[0]
    zeros = lambda *s: jnp.zeros(s, F32)
    pad_rows = lambda a: jnp.pad(a, ((0, 0), (CONV_PAD - (CONV_W - 1), 0), (0, 0)))
    yp, sp = _layer(x_prompt, zeros(bp, CONV_PAD, QKV_GROUPS * LANES), zeros(bp, HEADS, HEAD_DIM, HEAD_DIM),
                    zeros(bp, CONV_PAD, D_MODEL), zeros(bp, LRU_BLOCKS, 1, LANES),
                    mix_w, post_w, moe_w, chunk=64, reset_first=True)
    ys, ss = _layer(x_sample, pad_rows(state_gdn_conv[0]), state_gdn[0], pad_rows(state_lru_conv[0]),
                    state_lru[0].reshape(bs, LRU_BLOCKS, 1, LANES),
                    mix_w, post_w, moe_w, chunk=x_sample.shape[1], reset_first=False)
    return (yp, ys) + tuple(a[None] for a in sp) + tuple(a[None] for a in ss)
```

```python
import functools
import math

import jax
import jax.numpy as jnp
from jax import lax
from jax.experimental import pallas as pl
from jax.experimental.pallas import tpu as pltpu

F32 = jnp.float32
I32 = jnp.int32
MXU_DTYPE = jnp.bfloat16

D_MODEL = 1024
HEADS = 8
HEAD_DIM = 128
QKV_GROUPS = 3 * HEADS
LRU_BLOCKS = 8
CONV_W = 4
CONV_PAD = 8
N_EXPERTS = 64
N_GROUPS = 8
GROUP_SIZE = N_EXPERTS // N_GROUPS
TOPK_GROUPS = 4
TOP_K = 8
EXPERT_FF = 256
SHARED_FF = 256
ROUTED_SCALE = 2.5
LRU_C = 8.0
DEPTH = 1
DN_ALPHA = (2.0 * DEPTH) ** 0.25
LN_EPS = 1e-5
RMS_EPS = 1e-6
L2_EPS = 1e-6

LANES = 128
SUBLANES = 8
BF16_ROWS = 16
VMEM_LIMIT_BYTES = 56 * 1024 * 1024

MIX_TILE = 256
HEAD_GROUP = 8
TOK_TILE = 256
ROW_BLOCK = 512


def _dot(a, b):
    return jnp.dot(a.astype(MXU_DTYPE), b.astype(MXU_DTYPE), preferred_element_type=F32)


def _dot_nt(a, b):
    return lax.dot_general(a.astype(MXU_DTYPE), b.astype(MXU_DTYPE),
                           (((1,), (1,)), ((), ())), preferred_element_type=F32)


def _dot_tn(a, b):
    return lax.dot_general(a.astype(MXU_DTYPE), b.astype(MXU_DTYPE),
                           (((0,), (0,)), ((), ())), preferred_element_type=F32)


def _dot_f32(a, b):
    return jnp.dot(a, b, precision=lax.Precision.HIGHEST, preferred_element_type=F32)


def _sigmoid(x):
    return 1.0 / (1.0 + jnp.exp(-x))


def _softplus(x):
    return jnp.maximum(x, 0.0) + jnp.log1p(jnp.exp(-jnp.abs(x)))


def _widen(col, width):
    if width <= LANES:
        return col[:, :width]
    return jnp.concatenate([col] * (width // LANES), axis=1)


def _mixer_kernel(x_ref, gbuf0_ref, s0_ref, lbuf0_ref, h0_ref,
                  wmain_ref, wba_ref, gcw_ref, alog_ref, dtb_ref, gnw_ref,
                  lcw_ref, lcb_ref, wax_ref, lbax_ref, lam_ref,
                  mix_ref, gbuf_ref, s_ref, lbuf_ref, h_ref,
                  qkv_s, z_s, lx_s, ly_s, col_s, row_s, o_s, hst_s, lsig_s,
                  *, tile, chunk, reset_first):
    TT, C = tile, chunk
    NC = TT // C
    t = pl.program_id(1)

    @pl.when(t == 0)
    def _load_state():
        for g in range(QKV_GROUPS):
            qkv_s[g, 0:CONV_PAD, :] = gbuf0_ref[0, :, g * LANES:(g + 1) * LANES]
        for k in range(LRU_BLOCKS):
            lx_s[k, 0:CONV_PAD, :] = lbuf0_ref[0, :, k * LANES:(k + 1) * LANES]
        s_ref[...] = s0_ref[...]
        hst_s[...] = h0_ref[0]

    xb = x_ref[0].astype(MXU_DTYPE)
    for gp in range(QKV_GROUPS // 2):
        res = jnp.dot(xb, wmain_ref[:, gp * 256:(gp + 1) * 256], preferred_element_type=F32)
        qkv_s[2 * gp, CONV_PAD:CONV_PAD + TT, :] = res[:, :LANES]
        qkv_s[2 * gp + 1, CONV_PAD:CONV_PAD + TT, :] = res[:, LANES:]
    col0 = QKV_GROUPS * LANES
    for gp in range(HEADS // 2):
        res = jnp.dot(xb, wmain_ref[:, col0 + gp * 256:col0 + (gp + 1) * 256], preferred_element_type=F32)
        z_s[2 * gp] = res[:, :LANES]
        z_s[2 * gp + 1] = res[:, LANES:]
    col0 += HEADS * LANES
    for gp in range(LRU_BLOCKS // 2):
        res = jnp.dot(xb, wmain_ref[:, col0 + gp * 256:col0 + (gp + 1) * 256], preferred_element_type=F32)
        lx_s[2 * gp, CONV_PAD:CONV_PAD + TT, :] = res[:, :LANES]
        lx_s[2 * gp + 1, CONV_PAD:CONV_PAD + TT, :] = res[:, LANES:]
    col0 += LRU_BLOCKS * LANES
    for gp in range(LRU_BLOCKS // 2):
        res = jnp.dot(xb, wmain_ref[:, col0 + gp * 256:col0 + (gp + 1) * 256], preferred_element_type=F32)
        ly_s[2 * gp] = res[:, :LANES]
        ly_s[2 * gp + 1] = res[:, LANES:]
    ba = jnp.dot(xb, wba_ref[...], preferred_element_type=F32)

    for g in range(QKV_GROUPS):
        gbuf_ref[0, :, g * LANES:(g + 1) * LANES] = qkv_s[g, TT:TT + CONV_PAD, :]
    for k in range(LRU_BLOCKS):
        lbuf_ref[0, :, k * LANES:(k + 1) * LANES] = lx_s[k, TT:TT + CONV_PAD, :]

    row = lax.broadcasted_iota(I32, (TT, TT), 0)
    colm = lax.broadcasted_iota(I32, (TT, TT), 1)
    same_chunk = (row // C) == (colm // C)
    causal = same_chunk & (row >= colm)
    strict = same_chunk & (row > colm)
    beta_all = _sigmoid(ba)
    g_all = -jnp.exp(alog_ref[...]) * _softplus(ba + dtb_ref[...])
    gc_all = _dot_f32(causal.astype(F32), g_all)
    gt_all = _dot_f32(same_chunk.astype(F32), g_all)
    gc_t = gc_all.T
    for h in range(HEADS):
        col_s[0, h] = jnp.broadcast_to(beta_all[:, h:h + 1], (TT, LANES))
        col_s[1, h] = jnp.broadcast_to(gc_all[:, HEADS + h:HEADS + h + 1], (TT, LANES))
        col_s[2, h] = jnp.broadcast_to(gt_all[:, HEADS + h:HEADS + h + 1], (TT, LANES))
        row_s[h] = gc_t[HEADS + h:HEADS + h + 1, :]

    def conv_silu(g):
        w = gcw_ref[g]
        acc = qkv_s[g, pl.ds(CONV_PAD - 3, TT), :] * w[0:1]
        acc = acc + qkv_s[g, pl.ds(CONV_PAD - 2, TT), :] * w[1:2]
        acc = acc + qkv_s[g, pl.ds(CONV_PAD - 1, TT), :] * w[2:3]
        acc = acc + qkv_s[g, pl.ds(CONV_PAD, TT), :] * w[3:4]
        return acc * _sigmoid(acc)

    n_levels = int(math.log2(C))
    cat = lambda parts, axis: parts[0] if len(parts) == 1 else jnp.concatenate(parts, axis=axis)
    G = TT // SUBLANES
    sub = lax.broadcasted_iota(I32, (G, SUBLANES, LANES), 1)
    first_row = lax.broadcasted_iota(I32, (TT, LANES), 0) == 0

    def lru_block(kb_):
        w = lcw_ref[kb_]
        xc = lx_s[kb_, pl.ds(CONV_PAD - 3, TT), :] * w[0:1]
        xc = xc + lx_s[kb_, pl.ds(CONV_PAD - 2, TT), :] * w[1:2]
        xc = xc + lx_s[kb_, pl.ds(CONV_PAD - 1, TT), :] * w[2:3]
        xc = xc + lx_s[kb_, pl.ds(CONV_PAD, TT), :] * w[3:4]
        xc = xc + lcb_ref[kb_]
        gates = _sigmoid(_dot(xc, wax_ref[kb_]) + lbax_ref[kb_])
        r = gates[:, :LANES]
        gi = gates[:, LANES:]
        log_a = (LRU_C * r) * lsig_s[kb_]
        a = jnp.exp(log_a)
        th = jnp.tanh(log_a)
        mult = jnp.sqrt(-2.0 * th / (1.0 - th))
        if reset_first:
            mult = jnp.where(first_row & (t == 0), 1.0, mult)
        hprev = hst_s[kb_]
        b = mult * gi * xc + jnp.where(first_row, a * hprev, 0.0)
        a3 = a.reshape(G, SUBLANES, LANES)
        b3 = b.reshape(G, SUBLANES, LANES)
        for s in (1, 2, 4):
            a_sh = jnp.where(sub >= s, pltpu.roll(a3, s, 1), 1.0)
            b_sh = jnp.where(sub >= s, pltpu.roll(b3, s, 1), 0.0)
            b3 = a3 * b_sh + b3
            a3 = a3 * a_sh
        rows = []
        hc = jnp.zeros((1, LANES), F32)
        for gidx in range(G):
            hgrp = b3[gidx] if gidx == 0 else a3[gidx] * hc + b3[gidx]
            hc = hgrp[SUBLANES - 1:SUBLANES, :]
            rows.append(hgrp)
        hst_s[kb_] = hc
        o_s[HEADS + kb_] = jnp.concatenate(rows, axis=0) * jax.nn.gelu(ly_s[kb_])

    lsig_s[...] = -_softplus(-lam_ref[...])
    n_groups = HEADS // HEAD_GROUP
    lru_per_group = LRU_BLOCKS // n_groups

    def head_group_body(hg, carry):
        hs = [hg * HEAD_GROUP + i for i in range(HEAD_GROUP)]
        R = range(HEAD_GROUP)
        lru_todo = [hg * lru_per_group + i for i in range(lru_per_group)]

        def lru_step():
            if lru_todo:
                lru_block(lru_todo.pop(0))

        q = [conv_silu(h) for h in hs]
        k = [conv_silu(HEADS + h) for h in hs]
        v = [conv_silu(2 * HEADS + h) for h in hs]
        q = [x * lax.rsqrt(jnp.sum(x * x, axis=-1, keepdims=True) + L2_EPS) * (HEAD_DIM ** -0.5) for x in q]
        k = [x * lax.rsqrt(jnp.sum(x * x, axis=-1, keepdims=True) + L2_EPS) for x in k]
        beta = [col_s[0, h] for h in hs]
        gcol = [col_s[1, h] for h in hs]
        gtot = [col_s[2, h] for h in hs]
        decay = [jnp.where(causal, jnp.exp(_widen(gcol[i], TT) - row_s[hs[i]]), 0.0) for i in R]
        eg = [jnp.exp(g) for g in gcol]
        kb = [k[i] * beta[i] for i in R]
        qks = [_dot_nt(jnp.concatenate([kb[i], q[i]], axis=0), k[i]) for i in R]
        nmat = [jnp.where(strict, -(qks[i][:TT] * decay[i]), 0.0) for i in R]
        qk = [qks[i][TT:] * decay[i] for i in R]
        lru_step()
        tm = nmat
        if n_levels > 1:
            npow = [_dot(n, n) for n in nmat]
            for j in range(1, n_levels):
                if j < n_levels - 1:
                    r2 = [_dot(jnp.concatenate([tm[i], npow[i]], axis=0), npow[i]) for i in R]
                    tm = [tm[i] + npow[i] + r2[i][:TT] for i in R]
                    npow = [r2[i][TT:] for i in R]
                else:
                    tm = [tm[i] + npow[i] + _dot(tm[i], npow[i]) for i in R]
                lru_step()
        rhs =[jnp.concatenate([kb[i] * eg[i], v[i] * beta[i]], axis=1) for i in R]
        wu = [rhs[i] + _dot(tm[i], rhs[i]) for i in R]
        qd = [q[i] * eg[i] for i in R]
        kd = [k[i] * jnp.exp(gtot[i] - gcol[i]) for i in R]
        egt = [jnp.exp(g) for g in gtot]
        kwu = [[_dot_tn(kd[i][c * C:(c + 1) * C], wu[i][c * C:(c + 1) * C]) for c in range(NC)] for i in R]
        state = [s_ref[0, h] for h in hs]
        starts = [[] for _ in R]
        for c in range(NC):
            for i in R:
                starts[i].append(state[i])
                m = kwu[i][c]
                state[i] = (state[i] * egt[i][c * C:c * C + 1, :] + m[:, LANES:]) - _dot(m[:, :LANES], state[i])
        for i in R:
            s_ref[0, hs[i]] = state[i]
        while lru_todo:
            lru_step()
        for i in R:
            wq =[_dot(jnp.concatenate([wu[i][c * C:(c + 1) * C, :LANES], qd[i][c * C:(c + 1) * C]], axis=0),
                       starts[i][c]) for c in range(NC)]
            vn = cat([wu[i][c * C:(c + 1) * C, LANES:] - wq[c][:C] for c in range(NC)], 0)
            o = cat([wq[c][C:] for c in range(NC)], 0) + _dot(qk[i], vn)
            o = o * lax.rsqrt(jnp.mean(o * o, axis=-1, keepdims=True) + RMS_EPS) * gnw_ref[...]
            zz = z_s[hs[i]]
            o_s[hs[i]] = o * (zz * _sigmoid(zz))
        return carry

    lax.fori_loop(0, HEADS // HEAD_GROUP, head_group_body, 0)

    for g in range(HEADS + LRU_BLOCKS):
        mix_ref[0, :, g * LANES:(g + 1) * LANES] = o_s[g].astype(mix_ref.dtype)
    h_ref[0] = hst_s[...]
    for g in range(QKV_GROUPS):
        qkv_s[g, 0:CONV_PAD, :] = qkv_s[g, TT:TT + CONV_PAD, :]
    for k in range(LRU_BLOCKS):
        lx_s[k, 0:CONV_PAD, :] = lx_s[k, TT:TT + CONV_PAD, :]


def _const_spec(shape):
    nd = len(shape)
    return pl.BlockSpec(shape, lambda b, t, _n=nd: (0,) * _n)


def _mixer(x, gbuf0, s0, lbuf0, h0, wts, *, chunk, reset_first, interpret=False):
    B, T, _ = x.shape
    TT = min(MIX_TILE, T)
    assert T % TT == 0 and TT % chunk == 0 and TT % SUBLANES == 0
    NT = T // TT
    kern = functools.partial(_mixer_kernel, tile=TT, chunk=chunk, reset_first=reset_first)
    in_specs = [
        pl.BlockSpec((1, TT, D_MODEL), lambda b, t: (b, t, 0)),
        pl.BlockSpec((1, CONV_PAD, QKV_GROUPS * LANES), lambda b, t: (b, 0, 0)),
        pl.BlockSpec((1, HEADS, HEAD_DIM, HEAD_DIM), lambda b, t: (b, 0, 0, 0)),
        pl.BlockSpec((1, CONV_PAD, LRU_BLOCKS * LANES), lambda b, t: (b, 0, 0)),
        pl.BlockSpec((1, LRU_BLOCKS, 1, LANES), lambda b, t: (b, 0, 0, 0)),
    ] + [_const_spec(w.shape) for w in wts]
    out_shape = (
        jax.ShapeDtypeStruct((B, T, (HEADS + LRU_BLOCKS) * LANES), MXU_DTYPE),
        jax.ShapeDtypeStruct((B, CONV_PAD, QKV_GROUPS * LANES), F32),
        jax.ShapeDtypeStruct((B, HEADS, HEAD_DIM, HEAD_DIM), F32),
        jax.ShapeDtypeStruct((B, CONV_PAD, LRU_BLOCKS * LANES), F32),
        jax.ShapeDtypeStruct((B, LRU_BLOCKS, 1, LANES), F32),
    )
    out_specs = (
        pl.BlockSpec((1, TT, (HEADS + LRU_BLOCKS) * LANES), lambda b, t: (b, t, 0)),
        pl.BlockSpec((1, CONV_PAD, QKV_GROUPS * LANES), lambda b, t: (b, 0, 0)),
        pl.BlockSpec((1, HEADS, HEAD_DIM, HEAD_DIM), lambda b, t: (b, 0, 0, 0)),
        pl.BlockSpec((1, CONV_PAD, LRU_BLOCKS * LANES), lambda b, t: (b, 0, 0)),
        pl.BlockSpec((1, LRU_BLOCKS, 1, LANES), lambda b, t: (b, 0, 0, 0)),
    )
    scratch = [
        pltpu.VMEM((QKV_GROUPS, TT + CONV_PAD, LANES), F32),
        pltpu.VMEM((HEADS, TT, LANES), F32),
        pltpu.VMEM((LRU_BLOCKS, TT + CONV_PAD, LANES), F32),
        pltpu.VMEM((LRU_BLOCKS, TT, LANES), F32),
        pltpu.VMEM((3, HEADS, TT, LANES), F32),
        pltpu.VMEM((HEADS, 1, TT), F32),
        pltpu.VMEM((HEADS + LRU_BLOCKS, TT, LANES), F32),
        pltpu.VMEM((LRU_BLOCKS, 1, LANES), F32),
        pltpu.VMEM((LRU_BLOCKS, 1, LANES), F32),
    ]
    return pl.pallas_call(
        kern, grid=(B, NT), in_specs=in_specs, out_specs=out_specs, out_shape=out_shape,
        scratch_shapes=scratch, name="mixer",
        compiler_params=pltpu.CompilerParams(
            dimension_semantics=("parallel", "arbitrary"), vmem_limit_bytes=VMEM_LIMIT_BYTES),
        interpret=interpret,
    )(x, gbuf0, s0, lbuf0, h0, *wts)


def _mixer_weights(w_in, gdn_conv_w, gdn_a_log, gdn_dt_bias, gdn_norm_w,
                   lru_conv_w, lru_conv_b, lru_wa, lru_ba, lru_wx, lru_bx, lru_lambda):
    qkvz = QKV_GROUPS * LANES + HEADS * LANES
    ba0 = qkvz
    lx0 = qkvz + 2 * HEADS
    wmain = jnp.concatenate([w_in[:, :qkvz], w_in[:, lx0:]], axis=1).astype(MXU_DTYPE)
    wba = jnp.pad(w_in[:, ba0:lx0], ((0, 0), (0, LANES - 2 * HEADS))).astype(MXU_DTYPE)
    gcw = gdn_conv_w.reshape(CONV_W, QKV_GROUPS, LANES).transpose(1, 0, 2)
    pad8 = (HEADS, LANES - 2 * HEADS)
    alog = jnp.pad(gdn_a_log, pad8).reshape(1, LANES)
    dtb = jnp.pad(gdn_dt_bias, pad8).reshape(1, LANES)
    gnw = gdn_norm_w.reshape(1, LANES)
    lcw = lru_conv_w.reshape(CONV_W, LRU_BLOCKS, LANES).transpose(1, 0, 2)
    lcb = lru_conv_b.reshape(LRU_BLOCKS, 1, LANES)
    wax = jnp.concatenate([lru_wa, lru_wx], axis=-1).astype(MXU_DTYPE)
    lbax = jnp.concatenate([lru_ba.reshape(LRU_BLOCKS, 1, LANES),
                            lru_bx.reshape(LRU_BLOCKS, 1, LANES)], axis=-1)
    lam = lru_lambda.reshape(LRU_BLOCKS, 1, LANES)
    return (wmain, wba, gcw, alog, dtb, gnw, lcw, lcb, wax, lbax, lam)


def _layer_norm(x, g, b):
    mu = jnp.mean(x, axis=-1, keepdims=True)
    xc = x - mu
    var = jnp.mean(xc * xc, axis=-1, keepdims=True)
    return xc * lax.rsqrt(var + LN_EPS) * g + b


def _max01(x):
    return jnp.max(jnp.max(x, axis=0, keepdims=True), axis=1, keepdims=True)


def _min01(x):
    return jnp.min(jnp.min(x, axis=0, keepdims=True), axis=1, keepdims=True)


def _sum01(x):
    return jnp.sum(jnp.sum(x, axis=0, keepdims=True), axis=1, keepdims=True)


def _post_kernel(mix_ref, x_ref, wout_ref, ln1g_ref, ln1b_ref, rwt_ref, rbias_ref, shgu_ref, shd_ref,
                 h1b_ref, base_ref, lpos_ref, gate_ref, pcnt_ref, *, tile):
    TT = tile
    hpre = DN_ALPHA * x_ref[...] + jnp.dot(mix_ref[...], wout_ref[...], preferred_element_type=F32)
    h1 = _layer_norm(hpre, ln1g_ref[...], ln1b_ref[...])
    h1b = h1.astype(MXU_DTYPE)
    h1b_ref[...] = h1b

    gu = jnp.dot(h1b, shgu_ref[...], preferred_element_type=F32)
    g_sh = gu[:, :SHARED_FF]
    hsh = (g_sh * _sigmoid(g_sh)) * gu[:, SHARED_FF:]
    base_ref[...] = DN_ALPHA * h1 + _dot(hsh, shd_ref[...])

    logits = lax.dot_general(rwt_ref[...], h1b, (((1,), (1,)), ((), ())), preferred_element_type=F32)
    scores = _sigmoid(logits)
    s3 = scores.reshape(N_GROUPS, GROUP_SIZE, TT)
    b3 = (scores + rbias_ref[...]).reshape(N_GROUPS, GROUP_SIZE, TT)
    member = lax.broadcasted_iota(I32, (N_GROUPS, GROUP_SIZE, TT), 1)
    group = lax.broadcasted_iota(I32, (N_GROUPS, GROUP_SIZE, TT), 0)
    expert = group * GROUP_SIZE + member
    neg_inf = jnp.float32(-jnp.inf)
    m1 = jnp.max(b3, axis=1, keepdims=True)
    first1 = jnp.min(jnp.where(b3 == m1, member, GROUP_SIZE), axis=1, keepdims=True)
    m2 = jnp.max(jnp.where(member == first1, neg_inf, b3), axis=1, keepdims=True)
    gs = m1 + m2
    gidx = lax.broadcasted_iota(I32, (N_GROUPS, 1, TT), 0)
    grank = jnp.zeros((N_GROUPS, 1, TT), I32)
    for go in range(N_GROUPS):
        other = gs[go:go + 1]
        beats = (other > gs) | ((other == gs) & (go < gidx))
        grank = grank + beats.astype(I32)
    masked = jnp.where(grank < TOPK_GROUPS, b3, neg_inf)
    onehots, gates = [], []
    for _ in range(TOP_K):
        mx = _max01(masked)
        first = _min01(jnp.where(masked == mx, expert, N_EXPERTS))
        hit = expert == first
        onehots.append(hit)
        gates.append(_sum01(jnp.where(hit, s3, 0.0)))
        masked = jnp.where(hit, neg_inf, masked)
    gsum = gates[0]
    for gk in gates[1:]:
        gsum = gsum + gk
    sel3 = jnp.zeros((N_GROUPS, GROUP_SIZE, TT), F32)
    for hit in onehots:
        sel3 = jnp.where(hit, 1.0, sel3)
    sel = sel3.reshape(N_EXPERTS, TT)
    tr = lax.broadcasted_iota(I32, (TT, TT), 0)
    tc = lax.broadcasted_iota(I32, (TT, TT), 1)
    rank = _dot(sel, (tr < tc).astype(F32))
    cnt = jnp.sum(sel, axis=1, keepdims=True)
    pcnt = jnp.floor((cnt + (BF16_ROWS - 1)) * (1.0 / BF16_ROWS)) * BF16_ROWS
    er = lax.broadcasted_iota(I32, (N_EXPERTS, N_EXPERTS), 0)
    ec = lax.broadcasted_iota(I32, (N_EXPERTS, N_EXPERTS), 1)
    pcnt_b = jnp.broadcast_to(pcnt, (N_EXPERTS, LANES))
    loff = _dot_f32((ec < er).astype(F32), pcnt_b)[:, 0:1]
    lpos3 = (loff + rank).reshape(N_GROUPS, GROUP_SIZE, TT)
    for k in range(TOP_K):
        lp = _sum01(jnp.where(onehots[k], lpos3, 0.0))
        lpos_ref[0, k:k + 1, :] = lp.reshape(1, TT).astype(I32)
        gate_ref[0, k:k + 1, :] = (gates[k] / gsum * ROUTED_SCALE).reshape(1, TT)
    pcnt_ref[0] = pcnt_b.astype(I32)


def _post(mix_in, x2, wts, *, interpret=False):
    N = x2.shape[0]
    TT = TOK_TILE
    assert N % TT == 0
    nt = N // TT
    tok = lambda w: pl.BlockSpec((TT, w), lambda i: (i, 0))
    const = lambda a: pl.BlockSpec(a.shape, lambda i, _n=a.ndim: (0,) * _n)
    out_shape = (
        jax.ShapeDtypeStruct((N, D_MODEL), MXU_DTYPE),
        jax.ShapeDtypeStruct((N, D_MODEL), F32),
        jax.ShapeDtypeStruct((nt, TOP_K, TT), I32),
        jax.ShapeDtypeStruct((nt, TOP_K, TT), F32),
        jax.ShapeDtypeStruct((nt, N_EXPERTS, LANES), I32),
    )
    out_specs = (
        tok(D_MODEL), tok(D_MODEL),
        pl.BlockSpec((1, TOP_K, TT), lambda i: (i, 0, 0)),
        pl.BlockSpec((1, TOP_K, TT), lambda i: (i, 0, 0)),
        pl.BlockSpec((1, N_EXPERTS, LANES), lambda i: (i, 0, 0)),
    )
    return pl.pallas_call(
        functools.partial(_post_kernel, tile=TT), grid=(nt,),
        in_specs=[tok(mix_in.shape[1]), tok(D_MODEL)] + [const(w) for w in wts],
        out_specs=out_specs, out_shape=out_shape, name="post",
        compiler_params=pltpu.CompilerParams(
            dimension_semantics=("parallel",), vmem_limit_bytes=VMEM_LIMIT_BYTES),
        interpret=interpret,
    )(mix_in, x2, *wts)


def _post_weights(w_out, ln1_g, ln1_b, router_w, router_bias, sh_w_gate, sh_w_up, sh_w_down):
    return (w_out.astype(MXU_DTYPE), ln1_g.reshape(1, D_MODEL), ln1_b.reshape(1, D_MODEL),
            router_w.T.astype(MXU_DTYPE), router_bias.reshape(N_EXPERTS, 1),
            jnp.concatenate([sh_w_gate, sh_w_up], axis=1).astype(MXU_DTYPE), sh_w_down.astype(MXU_DTYPE))


SORT_CHUNK = 512


def _local_rows(tile):
    worst = tile * TOP_K + N_EXPERTS * (BF16_ROWS - 1)
    return -(-worst // SORT_CHUNK) * SORT_CHUNK


def _run_copy(loc_ref, hbm_ref, sem, loc_row, hbm_row, to_hbm):
    loc = loc_ref.at[pl.ds(pl.multiple_of(loc_row, BF16_ROWS), BF16_ROWS), :]
    hbm = hbm_ref.at[pl.ds(pl.multiple_of(hbm_row, BF16_ROWS), BF16_ROWS), :]
    return pltpu.make_async_copy(loc, hbm, sem) if to_hbm else pltpu.make_async_copy(hbm, loc, sem)


def _move_runs(tab_ref, loc_ref, hbm_ref, sem, to_hbm):
    def per_expert(e, off):
        n = tab_ref[0, 0, e] // BF16_ROWS
        dst0 = tab_ref[0, 0, N_EXPERTS + e]

        def per_granule(j, c):
            _run_copy(loc_ref, hbm_ref, sem, off + j * BF16_ROWS, dst0 + j * BF16_ROWS, to_hbm).start()
            return c

        lax.fori_loop(0, n, per_granule, 0)
        return off + n * BF16_ROWS

    total = lax.fori_loop(0, N_EXPERTS, per_expert, 0)

    def wait_one(j, c):
        _run_copy(loc_ref, hbm_ref, sem, 0, 0, to_hbm).wait()
        return c

    lax.fori_loop(0, total // BF16_ROWS, wait_one, 0)


def _dispatch_kernel(tab_ref, tail_ref, h1b_ref, lpos_ref, xs_ref, xloc, zbuf, sem, *, tile):
    TT = tile
    LR = _local_rows(TT)
    xb = h1b_ref[...]
    lp = lpos_ref[0]
    for rc in range(LR // SORT_CHUNK):
        rows = lax.broadcasted_iota(I32, (SORT_CHUNK, TT), 0) + rc * SORT_CHUNK
        p = jnp.zeros((SORT_CHUNK, TT), F32)
        for k in range(TOP_K):
            p = jnp.where(rows == lp[k:k + 1, :], 1.0, p)
        xl = jnp.dot(p.astype(MXU_DTYPE), xb, preferred_element_type=F32)
        xloc[rc * SORT_CHUNK:(rc + 1) * SORT_CHUNK, :] = xl.astype(xloc.dtype)
    _move_runs(tab_ref, xloc, xs_ref, sem, True)

    @pl.when(pl.program_id(0) == pl.num_programs(0) - 1)
    def _zero_tails():
        zbuf[...] = jnp.zeros_like(zbuf)

        def per_expert(e, tot):
            n = tail_ref[0, 0, N_EXPERTS + e]
            start = tail_ref[0, 0, e]

            def per_granule(j, c):
                _run_copy(zbuf, xs_ref, sem, 0, start + j * BF16_ROWS, True).start()
                return c

            lax.fori_loop(0, n, per_granule, 0)
            return tot + n

        total = lax.fori_loop(0, N_EXPERTS, per_expert, 0)

        def wait_one(j, c):
            _run_copy(zbuf, xs_ref, sem, 0, 0, True).wait()
            return c

        lax.fori_loop(0, total, wait_one, 0)


def _dispatch(h1b, lpos, tab, tail, n_rows, *, interpret=False):
    N = h1b.shape[0]
    TT = TOK_TILE
    nt = N // TT
    return pl.pallas_call(
        functools.partial(_dispatch_kernel, tile=TT), grid=(nt,),
        in_specs=[
            pl.BlockSpec((1, 1, 2 * N_EXPERTS), lambda i: (i, 0, 0), memory_space=pltpu.SMEM),
            pl.BlockSpec((1, 1, 2 * N_EXPERTS), lambda i: (0, 0, 0), memory_space=pltpu.SMEM),
            pl.BlockSpec((TT, D_MODEL), lambda i: (i, 0)),
            pl.BlockSpec((1, TOP_K, TT), lambda i: (i, 0, 0)),
        ],
        out_specs=pl.BlockSpec(memory_space=pl.ANY),
        out_shape=jax.ShapeDtypeStruct((n_rows, D_MODEL), MXU_DTYPE),
        scratch_shapes=[pltpu.VMEM((_local_rows(TT), D_MODEL), MXU_DTYPE),
                        pltpu.VMEM((BF16_ROWS, D_MODEL), MXU_DTYPE),
                        pltpu.SemaphoreType.DMA(())],
        name="dispatch",
        compiler_params=pltpu.CompilerParams(
            dimension_semantics=("arbitrary",), vmem_limit_bytes=VMEM_LIMIT_BYTES),
        interpret=interpret,
    )(tab, tail, h1b, lpos)


def _expert_kernel(blk_e_ref, nact_ref, x_ref, wgu_ref, wd_ref, y_ref):
    j = pl.program_id(0)

    @pl.when(j < nact_ref[0])
    def _compute():
        gu = jnp.dot(x_ref[...], wgu_ref[0], preferred_element_type=F32)
        g = gu[:, :EXPERT_FF]
        h = (g * _sigmoid(g)) * gu[:, EXPERT_FF:]
        y_ref[...] = _dot(h, wd_ref[0]).astype(y_ref.dtype)

    @pl.when(j >= nact_ref[0])
    def _idle():
        y_ref[...] = jnp.zeros_like(y_ref)


def _experts(xs, wgu, wd, blk_e, nact, row_block, *, interpret=False):
    BM = row_block
    nblk = xs.shape[0] // BM
    grid_spec = pltpu.PrefetchScalarGridSpec(
        num_scalar_prefetch=2, grid=(nblk,),
        in_specs=[
            pl.BlockSpec((BM, D_MODEL), lambda j, be, na: (jnp.minimum(j, na[0] - 1), 0)),
            pl.BlockSpec((1, D_MODEL, 2 * EXPERT_FF), lambda j, be, na: (be[j], 0, 0)),
            pl.BlockSpec((1, EXPERT_FF, D_MODEL), lambda j, be, na: (be[j], 0, 0)),
        ],
        out_specs=pl.BlockSpec((BM, D_MODEL), lambda j, be, na: (jnp.where(j < na[0], j, nblk), 0)),
    )
    return pl.pallas_call(
        _expert_kernel, grid_spec=grid_spec,
        out_shape=jax.ShapeDtypeStruct(((nblk + 1) * BM, D_MODEL), MXU_DTYPE), name="experts",
        compiler_params=pltpu.CompilerParams(
            dimension_semantics=("arbitrary",), vmem_limit_bytes=VMEM_LIMIT_BYTES),
        interpret=interpret,
    )(blk_e, nact, xs, wgu, wd)


def _combine_kernel(tab_ref, lpos_ref, gate_ref, base_ref, ln2g_ref, ln2b_ref, ys_ref, out_ref, yloc, sem,
                    *, tile):
    TT = tile
    LR = _local_rows(TT)

    @pl.when(pl.program_id(0) == 0)
    def _init():
        yloc[...] = jnp.zeros_like(yloc)

    _move_runs(tab_ref, yloc, ys_ref, sem, False)
    lp = lpos_ref[0]
    gt = gate_ref[0]
    acc = base_ref[...]
    for rc in range(LR // SORT_CHUNK):
        rows = lax.broadcasted_iota(I32, (SORT_CHUNK, TT), 0) + rc * SORT_CHUNK
        w = jnp.zeros((SORT_CHUNK, TT), F32)
        for k in range(TOP_K):
            w = jnp.where(rows == lp[k:k + 1, :], gt[k:k + 1, :], w)
        acc = acc + _dot_tn(w, yloc[rc * SORT_CHUNK:(rc + 1) * SORT_CHUNK, :])
    out_ref[...] = _layer_norm(acc, ln2g_ref[...], ln2b_ref[...])


def _combine(ys, lpos, gate, base, ln2g, ln2b, tab, *, interpret=False):
    N = base.shape[0]
    TT = TOK_TILE
    nt = N // TT
    return pl.pallas_call(
        functools.partial(_combine_kernel, tile=TT), grid=(nt,),
        in_specs=[
            pl.BlockSpec((1, 1, 2 * N_EXPERTS), lambda i: (i, 0, 0), memory_space=pltpu.SMEM),
            pl.BlockSpec((1, TOP_K, TT), lambda i: (i, 0, 0)),
            pl.BlockSpec((1, TOP_K, TT), lambda i: (i, 0, 0)),
            pl.BlockSpec((TT, D_MODEL), lambda i: (i, 0)),
            pl.BlockSpec((1, D_MODEL), lambda i: (0, 0)),
            pl.BlockSpec((1, D_MODEL), lambda i: (0, 0)),
            pl.BlockSpec(memory_space=pl.ANY),
        ],
        out_specs=pl.BlockSpec((TT, D_MODEL), lambda i: (i, 0)),
        out_shape=jax.ShapeDtypeStruct((N, D_MODEL), F32),
        scratch_shapes=[pltpu.VMEM((_local_rows(TT), D_MODEL), MXU_DTYPE), pltpu.SemaphoreType.DMA(())],
        name="combine",
        compiler_params=pltpu.CompilerParams(
            dimension_semantics=("arbitrary",), vmem_limit_bytes=VMEM_LIMIT_BYTES),
        interpret=interpret,
    )(tab, lpos, gate, base, ln2g, ln2b, ys)


def _moe_weights(exp_w_gate, exp_w_up, exp_w_down, ln2_g, ln2_b):
    return (jnp.concatenate([exp_w_gate, exp_w_up], axis=-1).astype(MXU_DTYPE), exp_w_down.astype(MXU_DTYPE),
            ln2_g.reshape(1, D_MODEL), ln2_b.reshape(1, D_MODEL))


def _row_block(n_tokens):
    rb = ROW_BLOCK
    while rb > LANES and n_tokens * TOP_K // N_EXPERTS < rb:
        rb //= 2
    return rb


def _sorted_layout(pcnt, rb):
    nt = pcnt.shape[0]
    run_len = jnp.sum(pcnt, axis=0)
    region = (run_len + rb - 1) // rb * rb
    region_end = jnp.cumsum(region)
    region_start = region_end - region
    gbase = region_start[None, :] + jnp.cumsum(pcnt, axis=0) - pcnt
    tab = jnp.concatenate([pcnt, gbase], axis=1).reshape(nt, 1, 2 * N_EXPERTS).astype(I32)
    tail = jnp.concatenate([region_start + run_len, (region - run_len) // BF16_ROWS]
                           ).reshape(1, 1, 2 * N_EXPERTS).astype(I32)
    max_rows = nt * (TOK_TILE * TOP_K + N_EXPERTS * (BF16_ROWS - 1)) + N_EXPERTS * (rb - BF16_ROWS)
    nblk = -(-max_rows // rb)
    blk_row = jnp.arange(nblk, dtype=I32) * rb
    blk_e = jnp.minimum(jnp.sum((region_end[None, :] <= blk_row[:, None]).astype(I32), axis=1), N_EXPERTS - 1)
    nact = (region_end[-1] // rb).reshape(1).astype(I32)
    return tab, tail, blk_e, nact, nblk * rb


def _layer(x, gbuf0, s0, lbuf0, h0, mix_w, post_w, moe_w, *, chunk, reset_first, interpret=False):
    B, T, D = x.shape
    mix_in, gbuf, s_new, lbuf, h_new = _mixer(x, gbuf0, s0, lbuf0, h0, mix_w, chunk=chunk,
                                              reset_first=reset_first, interpret=interpret)
    N = B * T
    h1b, base, lpos, gate, pcnt = _post(mix_in.reshape(N, -1), x.reshape(N, D), post_w, interpret=interpret)
    rb = _row_block(N)
    tab, tail, blk_e, nact, n_rows = _sorted_layout(pcnt[:, :, 0], rb)
    wgu, wd, ln2g, ln2b = moe_w
    xs = _dispatch(h1b, lpos, tab, tail, n_rows, interpret=interpret)
    ys = _experts(xs, wgu, wd, blk_e, nact, rb, interpret=interpret)
    y = _combine(ys, lpos, gate, base, ln2g, ln2b, tab, interpret=interpret)
    keep = CONV_PAD - (CONV_W - 1)
    states = (gbuf[:, keep:], s_new, lbuf[:, keep:], h_new.reshape(B, D))
    return y.reshape(B, T, D), states


def kernel(x_prompt, x_sample, state_gdn_conv, state_gdn, state_lru_conv, state_lru, w_in, gdn_conv_w, gdn_a_log, gdn_dt_bias, gdn_norm_w, lru_conv_w, lru_conv_b, lru_wa, lru_ba, lru_wx, lru_bx, lru_lambda, w_out, ln1_g, ln1_b, router_w, router_bias, exp_w_gate, exp_w_up, exp_w_down, sh_w_gate, sh_w_up, sh_w_down, ln2_g, ln2_b):
    mix_w = _mixer_weights(w_in[0], gdn_conv_w[0], gdn_a_log[0], gdn_dt_bias[0], gdn_norm_w[0], lru_conv_w[0],
                           lru_conv_b[0], lru_wa[0], lru_ba[0], lru_wx[0], lru_bx[0], lru_lambda[0])
    post_w = _post_weights(w_out[0], ln1_g[0], ln1_b[0], router_w[0], router_bias[0],
                           sh_w_gate[0], sh_w_up[0], sh_w_down[0])
    moe_w = _moe_weights(exp_w_gate[0], exp_w_up[0], exp_w_down[0], ln2_g[0], ln2_b[0])
    bp, bs = x_prompt.shape[0], x_sample.shape[0]
    zeros = lambda *s: jnp.zeros(s, F32)
    pad_rows = lambda a: jnp.pad(a, ((0, 0), (CONV_PAD - (CONV_W - 1), 0), (0, 0)))
    yp, sp = _layer(x_prompt, zeros(bp, CONV_PAD, QKV_GROUPS * LANES), zeros(bp, HEADS, HEAD_DIM, HEAD_DIM),
                    zeros(bp, CONV_PAD, D_MODEL), zeros(bp, LRU_BLOCKS, 1, LANES),
                    mix_w, post_w, moe_w, chunk=64, reset_first=True)
    ys, ss = _layer(x_sample, pad_rows(state_gdn_conv[0]), state_gdn[0], pad_rows(state_lru_conv[0]),
                    state_lru[0].reshape(bs, LRU_BLOCKS, 1, LANES),
                    mix_w, post_w, moe_w, chunk=x_sample.shape[1], reset_first=False)
    return (yp, ys) + tuple(a[None] for a in sp) + tuple(a[None] for a in ss)
```

```python
import functools
import math

import jax
import jax.numpy as jnp
from jax import lax
from jax.experimental import pallas as pl
from jax.experimental.pallas import tpu as pltpu

F32 = jnp.float32
I32 = jnp.int32
MXU_DTYPE = jnp.bfloat16

D_MODEL = 1024
HEADS = 8
HEAD_DIM = 128
QKV_GROUPS = 3 * HEADS
LRU_BLOCKS = 8
CONV_W = 4
CONV_PAD = 8
N_EXPERTS = 64
N_GROUPS = 8
GROUP_SIZE = N_EXPERTS // N_GROUPS
TOPK_GROUPS = 4
TOP_K = 8
EXPERT_FF = 256
SHARED_FF = 256
ROUTED_SCALE = 2.5
LRU_C = 8.0
DEPTH = 1
DN_ALPHA = (2.0 * DEPTH) ** 0.25
LN_EPS = 1e-5
RMS_EPS = 1e-6
L2_EPS = 1e-6

LANES = 128
SUBLANES = 8
BF16_ROWS = 16
VMEM_LIMIT_BYTES = 56 * 1024 * 1024

MIX_TILE = 256
HEAD_GROUP = 8
TOK_TILE = 256
ROW_BLOCK = 512


def _dot(a, b):
    return jnp.dot(a.astype(MXU_DTYPE), b.astype(MXU_DTYPE), preferred_element_type=F32)


def _dot_nt(a, b):
    return lax.dot_general(a.astype(MXU_DTYPE), b.astype(MXU_DTYPE),
                           (((1,), (1,)), ((), ())), preferred_element_type=F32)


def _dot_tn(a, b):
    return lax.dot_general(a.astype(MXU_DTYPE), b.astype(MXU_DTYPE),
                           (((0,), (0,)), ((), ())), preferred_element_type=F32)


def _dot_f32(a, b):
    return jnp.dot(a, b, precision=lax.Precision.HIGHEST, preferred_element_type=F32)


def _sigmoid(x):
    return 1.0 / (1.0 + jnp.exp(-x))


def _softplus(x):
    return jnp.maximum(x, 0.0) + jnp.log1p(jnp.exp(-jnp.abs(x)))


def _widen(col, width):
    if width <= LANES:
        return col[:, :width]
    return jnp.concatenate([col] * (width // LANES), axis=1)


def _mixer_kernel(x_ref, gbuf0_ref, s0_ref, lbuf0_ref, h0_ref,
                  wmain_ref, wba_ref, gcw_ref, alog_ref, dtb_ref, gnw_ref,
                  lcw_ref, lcb_ref, wax_ref, lbax_ref, lam_ref,
                  mix_ref, gbuf_ref, s_ref, lbuf_ref, h_ref,
                  qkv_s, z_s, lx_s, ly_s, col_s, row_s, o_s, hst_s, lsig_s,
                  *, tile, chunk, reset_first):
    TT, C = tile, chunk
    NC = TT // C
    t = pl.program_id(1)

    @pl.when(t == 0)
    def _load_state():
        for g in range(QKV_GROUPS):
            qkv_s[g, 0:CONV_PAD, :] = gbuf0_ref[0, :, g * LANES:(g + 1) * LANES]
        for k in range(LRU_BLOCKS):
            lx_s[k, 0:CONV_PAD, :] = lbuf0_ref[0, :, k * LANES:(k + 1) * LANES]
        s_ref[...] = s0_ref[...]
        hst_s[...] = h0_ref[0]

    xb = x_ref[0].astype(MXU_DTYPE)
    for gp in range(QKV_GROUPS // 2):
        res = jnp.dot(xb, wmain_ref[:, gp * 256:(gp + 1) * 256], preferred_element_type=F32)
        qkv_s[2 * gp, CONV_PAD:CONV_PAD + TT, :] = res[:, :LANES]
        qkv_s[2 * gp + 1, CONV_PAD:CONV_PAD + TT, :] = res[:, LANES:]
    col0 = QKV_GROUPS * LANES
    for gp in range(HEADS // 2):
        res = jnp.dot(xb, wmain_ref[:, col0 + gp * 256:col0 + (gp + 1) * 256], preferred_element_type=F32)
        z_s[2 * gp] = res[:, :LANES]
        z_s[2 * gp + 1] = res[:, LANES:]
    col0 += HEADS * LANES
    for gp in range(LRU_BLOCKS // 2):
        res = jnp.dot(xb, wmain_ref[:, col0 + gp * 256:col0 + (gp + 1) * 256], preferred_element_type=F32)
        lx_s[2 * gp, CONV_PAD:CONV_PAD + TT, :] = res[:, :LANES]
        lx_s[2 * gp + 1, CONV_PAD:CONV_PAD + TT, :] = res[:, LANES:]
    col0 += LRU_BLOCKS * LANES
    for gp in range(LRU_BLOCKS // 2):
        res = jnp.dot(xb, wmain_ref[:, col0 + gp * 256:col0 + (gp + 1) * 256], preferred_element_type=F32)
        ly_s[2 * gp] = res[:, :LANES]
        ly_s[2 * gp + 1] = res[:, LANES:]
    ba = jnp.dot(xb, wba_ref[...], preferred_element_type=F32)

    for g in range(QKV_GROUPS):
        gbuf_ref[0, :, g * LANES:(g + 1) * LANES] = qkv_s[g, TT:TT + CONV_PAD, :]
    for k in range(LRU_BLOCKS):
        lbuf_ref[0, :, k * LANES:(k + 1) * LANES] = lx_s[k, TT:TT + CONV_PAD, :]

    row = lax.broadcasted_iota(I32, (TT, TT), 0)
    colm = lax.broadcasted_iota(I32, (TT, TT), 1)
    same_chunk = (row // C) == (colm // C)
    causal = same_chunk & (row >= colm)
    strict = same_chunk & (row > colm)
    beta_all = _sigmoid(ba)
    g_all = -jnp.exp(alog_ref[...]) * _softplus(ba + dtb_ref[...])
    gc_all = _dot_f32(causal.astype(F32), g_all)
    gt_all = _dot_f32(same_chunk.astype(F32), g_all)
    gc_t = gc_all.T
    for h in range(HEADS):
        col_s[0, h] = jnp.broadcast_to(beta_all[:, h:h + 1], (TT, LANES))
        col_s[1, h] = jnp.broadcast_to(gc_all[:, HEADS + h:HEADS + h + 1], (TT, LANES))
        col_s[2, h] = jnp.broadcast_to(gt_all[:, HEADS + h:HEADS + h + 1], (TT, LANES))
        row_s[h] = gc_t[HEADS + h:HEADS + h + 1, :]

    def conv_silu(g):
        w = gcw_ref[g]
        acc = qkv_s[g, pl.ds(CONV_PAD - 3, TT), :] * w[0:1]
        acc = acc + qkv_s[g, pl.ds(CONV_PAD - 2, TT), :] * w[1:2]
        acc = acc + qkv_s[g, pl.ds(CONV_PAD - 1, TT), :] * w[2:3]
        acc = acc + qkv_s[g, pl.ds(CONV_PAD, TT), :] * w[3:4]
        return acc * _sigmoid(acc)

    n_levels = int(math.log2(C))
    cat = lambda parts, axis: parts[0] if len(parts) == 1 else jnp.concatenate(parts, axis=axis)
    G = TT // SUBLANES
    sub = lax.broadcasted_iota(I32, (G, SUBLANES, LANES), 1)
    first_row = lax.broadcasted_iota(I32, (TT, LANES), 0) == 0

    def lru_block(kb_):
        w = lcw_ref[kb_]
        xc = lx_s[kb_, pl.ds(CONV_PAD - 3, TT), :] * w[0:1]
        xc = xc + lx_s[kb_, pl.ds(CONV_PAD - 2, TT), :] * w[1:2]
        xc = xc + lx_s[kb_, pl.ds(CONV_PAD - 1, TT), :] * w[2:3]
        xc = xc + lx_s[kb_, pl.ds(CONV_PAD, TT), :] * w[3:4]
        xc = xc + lcb_ref[kb_]
        gates = _sigmoid(_dot(xc, wax_ref[kb_]) + lbax_ref[kb_])
        r = gates[:, :LANES]
        gi = gates[:, LANES:]
        log_a = (LRU_C * r) * lsig_s[kb_]
        a = jnp.exp(log_a)
        th = jnp.tanh(log_a)
        mult = jnp.sqrt(-2.0 * th / (1.0 - th))
        if reset_first:
            mult = jnp.where(first_row & (t == 0), 1.0, mult)
        hprev = hst_s[kb_]
        b = mult * gi * xc + jnp.where(first_row, a * hprev, 0.0)
        a3 = a.reshape(G, SUBLANES, LANES)
        b3 = b.reshape(G, SUBLANES, LANES)
        for s in (1, 2, 4):
            a_sh = jnp.where(sub >= s, pltpu.roll(a3, s, 1), 1.0)
            b_sh = jnp.where(sub >= s, pltpu.roll(b3, s, 1), 0.0)
            b3 = a3 * b_sh + b3
            a3 = a3 * a_sh
        rows = []
        hc = jnp.zeros((1, LANES), F32)
        for gidx in range(G):
            hgrp = b3[gidx] if gidx == 0 else a3[gidx] * hc + b3[gidx]
            hc = hgrp[SUBLANES - 1:SUBLANES, :]
            rows.append(hgrp)
        hst_s[kb_] = hc
        o_s[HEADS + kb_] = jnp.concatenate(rows, axis=0) * jax.nn.gelu(ly_s[kb_])

    lsig_s[...] = -_softplus(-lam_ref[...])
    n_groups = HEADS // HEAD_GROUP
    lru_per_group = LRU_BLOCKS // n_groups

    def head_group_body(hg, carry):
        hs = [hg * HEAD_GROUP + i for i in range(HEAD_GROUP)]
        R = range(HEAD_GROUP)
        lru_todo = [hg * lru_per_group + i for i in range(lru_per_group)]

        def lru_step():
            if lru_todo:
                lru_block(lru_todo.pop(0))

        q = [conv_silu(h) for h in hs]
        k = [conv_silu(HEADS + h) for h in hs]
        v = [conv_silu(2 * HEADS + h) for h in hs]
        q = [x * lax.rsqrt(jnp.sum(x * x, axis=-1, keepdims=True) + L2_EPS) * (HEAD_DIM ** -0.5) for x in q]
        k = [x * lax.rsqrt(jnp.sum(x * x, axis=-1, keepdims=True) + L2_EPS) for x in k]
        beta = [col_s[0, h] for h in hs]
        gcol = [col_s[1, h] for h in hs]
        gtot = [col_s[2, h] for h in hs]
        decay = [jnp.where(causal, jnp.exp(_widen(gcol[i], TT) - row_s[hs[i]]), 0.0) for i in R]
        eg = [jnp.exp(g) for g in gcol]
        kb = [k[i] * beta[i] for i in R]
        qks = [_dot_nt(jnp.concatenate([kb[i], q[i]], axis=0), k[i]) for i in R]
        nmat = [jnp.where(strict, -(qks[i][:TT] * decay[i]), 0.0) for i in R]
        qk = [qks[i][TT:] * decay[i] for i in R]
        lru_step()
        tm = nmat
        if n_levels > 1:
            npow = [_dot(n, n) for n in nmat]
            for j in range(1, n_levels):
                if j < n_levels - 1:
                    r2 = [_dot(jnp.concatenate([tm[i], npow[i]], axis=0), npow[i]) for i in R]
                    tm = [tm[i] + npow[i] + r2[i][:TT] for i in R]
                    npow = [r2[i][TT:] for i in R]
                else:
                    tm = [tm[i] + npow[i] + _dot(tm[i], npow[i]) for i in R]
                lru_step()
        rhs =[jnp.concatenate([kb[i] * eg[i], v[i] * beta[i]], axis=1) for i in R]
        wu = [rhs[i] + _dot(tm[i], rhs[i]) for i in R]
        qd = [q[i] * eg[i] for i in R]
        kd = [k[i] * jnp.exp(gtot[i] - gcol[i]) for i in R]
        egt = [jnp.exp(g) for g in gtot]
        kwu = [[_dot_tn(kd[i][c * C:(c + 1) * C], wu[i][c * C:(c + 1) * C]) for c in range(NC)] for i in R]
        state = [s_ref[0, h] for h in hs]
        starts = [[] for _ in R]
        for c in range(NC):
            for i in R:
                starts[i].append(state[i])
                m = kwu[i][c]
                state[i] = (state[i] * egt[i][c * C:c * C + 1, :] + m[:, LANES:]) - _dot(m[:, :LANES], state[i])
        for i in R:
            s_ref[0, hs[i]] = state[i]
        while lru_todo:
            lru_step()
        for i in R:
            wq =[_dot(jnp.concatenate([wu[i][c * C:(c + 1) * C, :LANES], qd[i][c * C:(c + 1) * C]], axis=0),
                       starts[i][c]) for c in range(NC)]
            vn = cat([wu[i][c * C:(c + 1) * C, LANES:] - wq[c][:C] for c in range(NC)], 0)
            o = cat([wq[c][C:] for c in range(NC)], 0) + _dot(qk[i], vn)
            o = o * lax.rsqrt(jnp.mean(o * o, axis=-1, keepdims=True) + RMS_EPS) * gnw_ref[...]
            zz = z_s[hs[i]]
            o_s[hs[i]] = o * (zz * _sigmoid(zz))
        return carry

    lax.fori_loop(0, HEADS // HEAD_GROUP, head_group_body, 0)

    for g in range(HEADS + LRU_BLOCKS):
        mix_ref[0, :, g * LANES:(g + 1) * LANES] = o_s[g].astype(mix_ref.dtype)
    h_ref[0] = hst_s[...]
    for g in range(QKV_GROUPS):
        qkv_s[g, 0:CONV_PAD, :] = qkv_s[g, TT:TT + CONV_PAD, :]
    for k in range(LRU_BLOCKS):
        lx_s[k, 0:CONV_PAD, :] = lx_s[k, TT:TT + CONV_PAD, :]


def _const_spec(shape):
    nd = len(shape)
    return pl.BlockSpec(shape, lambda b, t, _n=nd: (0,) * _n)


def _mixer(x, gbuf0, s0, lbuf0, h0, wts, *, chunk, reset_first, interpret=False):
    B, T, _ = x.shape
    TT = min(MIX_TILE, T)
    assert T % TT == 0 and TT % chunk == 0 and TT % SUBLANES == 0
    NT = T // TT
    kern = functools.partial(_mixer_kernel, tile=TT, chunk=chunk, reset_first=reset_first)
    in_specs = [
        pl.BlockSpec((1, TT, D_MODEL), lambda b, t: (b, t, 0)),
        pl.BlockSpec((1, CONV_PAD, QKV_GROUPS * LANES), lambda b, t: (b, 0, 0)),
        pl.BlockSpec((1, HEADS, HEAD_DIM, HEAD_DIM), lambda b, t: (b, 0, 0, 0)),
        pl.BlockSpec((1, CONV_PAD, LRU_BLOCKS * LANES), lambda b, t: (b, 0, 0)),
        pl.BlockSpec((1, LRU_BLOCKS, 1, LANES), lambda b, t: (b, 0, 0, 0)),
    ] + [_const_spec(w.shape) for w in wts]
    out_shape = (
        jax.ShapeDtypeStruct((B, T, (HEADS + LRU_BLOCKS) * LANES), MXU_DTYPE),
        jax.ShapeDtypeStruct((B, CONV_PAD, QKV_GROUPS * LANES), F32),
        jax.ShapeDtypeStruct((B, HEADS, HEAD_DIM, HEAD_DIM), F32),
        jax.ShapeDtypeStruct((B, CONV_PAD, LRU_BLOCKS * LANES), F32),
        jax.ShapeDtypeStruct((B, LRU_BLOCKS, 1, LANES), F32),
    )
    out_specs = (
        pl.BlockSpec((1, TT, (HEADS + LRU_BLOCKS) * LANES), lambda b, t: (b, t, 0)),
        pl.BlockSpec((1, CONV_PAD, QKV_GROUPS * LANES), lambda b, t: (b, 0, 0)),
        pl.BlockSpec((1, HEADS, HEAD_DIM, HEAD_DIM), lambda b, t: (b, 0, 0, 0)),
        pl.BlockSpec((1, CONV_PAD, LRU_BLOCKS * LANES), lambda b, t: (b, 0, 0)),
        pl.BlockSpec((1, LRU_BLOCKS, 1, LANES), lambda b, t: (b, 0, 0, 0)),
    )
    scratch = [
        pltpu.VMEM((QKV_GROUPS, TT + CONV_PAD, LANES), F32),
        pltpu.VMEM((HEADS, TT, LANES), F32),
        pltpu.VMEM((LRU_BLOCKS, TT + CONV_PAD, LANES), F32),
        pltpu.VMEM((LRU_BLOCKS, TT, LANES), F32),
        pltpu.VMEM((3, HEADS, TT, LANES), F32),
        pltpu.VMEM((HEADS, 1, TT), F32),
        pltpu.VMEM((HEADS + LRU_BLOCKS, TT, LANES), F32),
        pltpu.VMEM((LRU_BLOCKS, 1, LANES), F32),
        pltpu.VMEM((LRU_BLOCKS, 1, LANES), F32),
    ]
    return pl.pallas_call(
        kern, grid=(B, NT), in_specs=in_specs, out_specs=out_specs, out_shape=out_shape,
        scratch_shapes=scratch, name="mixer",
        compiler_params=pltpu.CompilerParams(
            dimension_semantics=("parallel", "arbitrary"), vmem_limit_bytes=VMEM_LIMIT_BYTES),
        interpret=interpret,
    )(x, gbuf0, s0, lbuf0, h0, *wts)


def _mixer_weights(w_in, gdn_conv_w, gdn_a_log, gdn_dt_bias, gdn_norm_w,
                   lru_conv_w, lru_conv_b, lru_wa, lru_ba, lru_wx, lru_bx, lru_lambda):
    qkvz = QKV_GROUPS * LANES + HEADS * LANES
    ba0 = qkvz
    lx0 = qkvz + 2 * HEADS
    wmain = jnp.concatenate([w_in[:, :qkvz], w_in[:, lx0:]], axis=1).astype(MXU_DTYPE)
    wba = jnp.pad(w_in[:, ba0:lx0], ((0, 0), (0, LANES - 2 * HEADS))).astype(MXU_DTYPE)
    gcw = gdn_conv_w.reshape(CONV_W, QKV_GROUPS, LANES).transpose(1, 0, 2)
    pad8 = (HEADS, LANES - 2 * HEADS)
    alog = jnp.pad(gdn_a_log, pad8).reshape(1, LANES)
    dtb = jnp.pad(gdn_dt_bias, pad8).reshape(1, LANES)
    gnw = gdn_norm_w.reshape(1, LANES)
    lcw = lru_conv_w.reshape(CONV_W, LRU_BLOCKS, LANES).transpose(1, 0, 2)
    lcb = lru_conv_b.reshape(LRU_BLOCKS, 1, LANES)
    wax = jnp.concatenate([lru_wa, lru_wx], axis=-1).astype(MXU_DTYPE)
    lbax = jnp.concatenate([lru_ba.reshape(LRU_BLOCKS, 1, LANES),
                            lru_bx.reshape(LRU_BLOCKS, 1, LANES)], axis=-1)
    lam = lru_lambda.reshape(LRU_BLOCKS, 1, LANES)
    return (wmain, wba, gcw, alog, dtb, gnw, lcw, lcb, wax, lbax, lam)


def _layer_norm(x, g, b):
    mu = jnp.mean(x, axis=-1, keepdims=True)
    xc = x - mu
    var = jnp.mean(xc * xc, axis=-1, keepdims=True)
    return xc * lax.rsqrt(var + LN_EPS) * g + b


def _max01(x):
    return jnp.max(jnp.max(x, axis=0, keepdims=True), axis=1, keepdims=True)


def _min01(x):
    return jnp.min(jnp.min(x, axis=0, keepdims=True), axis=1, keepdims=True)


def _sum01(x):
    return jnp.sum(jnp.sum(x, axis=0, keepdims=True), axis=1, keepdims=True)


def _post_kernel(mix_ref, x_ref, wout_ref, ln1g_ref, ln1b_ref, rwt_ref, rbias_ref, shgu_ref, shd_ref,
                 h1b_ref, base_ref, lpos_ref, gate_ref, pcnt_ref, *, tile):
    TT = tile
    hpre = DN_ALPHA * x_ref[...] + jnp.dot(mix_ref[...], wout_ref[...], preferred_element_type=F32)
    h1 = _layer_norm(hpre, ln1g_ref[...], ln1b_ref[...])
    h1b = h1.astype(MXU_DTYPE)
    h1b_ref[...] = h1b

    gu = jnp.dot(h1b, shgu_ref[...], preferred_element_type=F32)
    g_sh = gu[:, :SHARED_FF]
    hsh = (g_sh * _sigmoid(g_sh)) * gu[:, SHARED_FF:]
    base_ref[...] = DN_ALPHA * h1 + _dot(hsh, shd_ref[...])

    logits = lax.dot_general(rwt_ref[...], h1b, (((1,), (1,)), ((), ())), preferred_element_type=F32)
    scores = _sigmoid(logits)
    s3 = scores.reshape(N_GROUPS, GROUP_SIZE, TT)
    b3 = (scores + rbias_ref[...]).reshape(N_GROUPS, GROUP_SIZE, TT)
    member = lax.broadcasted_iota(I32, (N_GROUPS, GROUP_SIZE, TT), 1)
    group = lax.broadcasted_iota(I32, (N_GROUPS, GROUP_SIZE, TT), 0)
    expert = group * GROUP_SIZE + member
    neg_inf = jnp.float32(-jnp.inf)
    m1 = jnp.max(b3, axis=1, keepdims=True)
    first1 = jnp.min(jnp.where(b3 == m1, member, GROUP_SIZE), axis=1, keepdims=True)
    m2 = jnp.max(jnp.where(member == first1, neg_inf, b3), axis=1, keepdims=True)
    gs = m1 + m2
    gidx = lax.broadcasted_iota(I32, (N_GROUPS, 1, TT), 0)
    grank = jnp.zeros((N_GROUPS, 1, TT), I32)
    for go in range(N_GROUPS):
        other = gs[go:go + 1]
        beats = (other > gs) | ((other == gs) & (go < gidx))
        grank = grank + beats.astype(I32)
    masked = jnp.where(grank < TOPK_GROUPS, b3, neg_inf)
    onehots, gates = [], []
    for _ in range(TOP_K):
        mx = _max01(masked)
        first = _min01(jnp.where(masked == mx, expert, N_EXPERTS))
        hit = expert == first
        onehots.append(hit)
        gates.append(_sum01(jnp.where(hit, s3, 0.0)))
        masked = jnp.where(hit, neg_inf, masked)
    gsum = gates[0]
    for gk in gates[1:]:
        gsum = gsum + gk
    sel3 = jnp.zeros((N_GROUPS, GROUP_SIZE, TT), F32)
    for hit in onehots:
        sel3 = jnp.where(hit, 1.0, sel3)
    sel = sel3.reshape(N_EXPERTS, TT)
    tr = lax.broadcasted_iota(I32, (TT, TT), 0)
    tc = lax.broadcasted_iota(I32, (TT, TT), 1)
    rank = _dot(sel, (tr < tc).astype(F32))
    cnt = jnp.sum(sel, axis=1, keepdims=True)
    pcnt = jnp.floor((cnt + (BF16_ROWS - 1)) * (1.0 / BF16_ROWS)) * BF16_ROWS
    er = lax.broadcasted_iota(I32, (N_EXPERTS, N_EXPERTS), 0)
    ec = lax.broadcasted_iota(I32, (N_EXPERTS, N_EXPERTS), 1)
    pcnt_b = jnp.broadcast_to(pcnt, (N_EXPERTS, LANES))
    loff = _dot_f32((ec < er).astype(F32), pcnt_b)[:, 0:1]
    lpos3 = (loff + rank).reshape(N_GROUPS, GROUP_SIZE, TT)
    for k in range(TOP_K):
        lp = _sum01(jnp.where(onehots[k], lpos3, 0.0))
        lpos_ref[0, k:k + 1, :] = lp.reshape(1, TT).astype(I32)
        gate_ref[0, k:k + 1, :] = (gates[k] / gsum * ROUTED_SCALE).reshape(1, TT)
    pcnt_ref[0] = pcnt_b.astype(I32)


def _post(mix_in, x2, wts, *, interpret=False):
    N = x2.shape[0]
    TT = TOK_TILE
    assert N % TT == 0
    nt = N // TT
    tok = lambda w: pl.BlockSpec((TT, w), lambda i: (i, 0))
    const = lambda a: pl.BlockSpec(a.shape, lambda i, _n=a.ndim: (0,) * _n)
    out_shape = (
        jax.ShapeDtypeStruct((N, D_MODEL), MXU_DTYPE),
        jax.ShapeDtypeStruct((N, D_MODEL), F32),
        jax.ShapeDtypeStruct((nt, TOP_K, TT), I32),
        jax.ShapeDtypeStruct((nt, TOP_K, TT), F32),
        jax.ShapeDtypeStruct((nt, N_EXPERTS, LANES), I32),
    )
    out_specs = (
        tok(D_MODEL), tok(D_MODEL),
        pl.BlockSpec((1, TOP_K, TT), lambda i: (i, 0, 0)),
        pl.BlockSpec((1, TOP_K, TT), lambda i: (i, 0, 0)),
        pl.BlockSpec((1, N_EXPERTS, LANES), lambda i: (i, 0, 0)),
    )
    return pl.pallas_call(
        functools.partial(_post_kernel, tile=TT), grid=(nt,),
        in_specs=[tok(mix_in.shape[1]), tok(D_MODEL)] + [const(w) for w in wts],
        out_specs=out_specs, out_shape=out_shape, name="post",
        compiler_params=pltpu.CompilerParams(
            dimension_semantics=("parallel",), vmem_limit_bytes=VMEM_LIMIT_BYTES),
        interpret=interpret,
    )(mix_in, x2, *wts)


def _post_weights(w_out, ln1_g, ln1_b, router_w, router_bias, sh_w_gate, sh_w_up, sh_w_down):
    return (w_out.astype(MXU_DTYPE), ln1_g.reshape(1, D_MODEL), ln1_b.reshape(1, D_MODEL),
            router_w.T.astype(MXU_DTYPE), router_bias.reshape(N_EXPERTS, 1),
            jnp.concatenate([sh_w_gate, sh_w_up], axis=1).astype(MXU_DTYPE), sh_w_down.astype(MXU_DTYPE))


SORT_CHUNK = 512


def _local_rows(tile):
    worst = tile * TOP_K + N_EXPERTS * (BF16_ROWS - 1)
    return -(-worst // SORT_CHUNK) * SORT_CHUNK


def _run_copy(loc_ref, hbm_ref, sem, loc_row, hbm_row, to_hbm):
    loc = loc_ref.at[pl.ds(pl.multiple_of(loc_row, BF16_ROWS), BF16_ROWS), :]
    hbm = hbm_ref.at[pl.ds(pl.multiple_of(hbm_row, BF16_ROWS), BF16_ROWS), :]
    return pltpu.make_async_copy(loc, hbm, sem) if to_hbm else pltpu.make_async_copy(hbm, loc, sem)


def _start_runs(tab_ref, loc_ref, hbm_ref, sem, to_hbm):
    def per_expert(e, off):
        n = tab_ref[0, 0, e] // BF16_ROWS
        dst0 = tab_ref[0, 0, N_EXPERTS + e]

        def per_granule(j, c):
            _run_copy(loc_ref, hbm_ref, sem, off + j * BF16_ROWS, dst0 + j * BF16_ROWS, to_hbm).start()
            return c

        lax.fori_loop(0, n, per_granule, 0)
        return off + n * BF16_ROWS

    return lax.fori_loop(0, N_EXPERTS, per_expert, 0)


def _wait_runs(rows, loc_ref, hbm_ref, sem, to_hbm):
    def wait_one(j, c):
        _run_copy(loc_ref, hbm_ref, sem, 0, 0, to_hbm).wait()
        return c

    lax.fori_loop(0, rows // BF16_ROWS, wait_one, 0)


def _move_runs(tab_ref, loc_ref, hbm_ref, sem, to_hbm):
    _wait_runs(_start_runs(tab_ref, loc_ref, hbm_ref, sem, to_hbm), loc_ref, hbm_ref, sem, to_hbm)


def _dispatch_kernel(tab_ref, tail_ref, h1b_ref, lpos_ref, xs_ref, xloc, zbuf, sem, *, tile):
    TT = tile
    LR = _local_rows(TT)
    xb = h1b_ref[...]
    lp = lpos_ref[0]
    for rc in range(LR // SORT_CHUNK):
        rows = lax.broadcasted_iota(I32, (SORT_CHUNK, TT), 0) + rc * SORT_CHUNK
        p = jnp.zeros((SORT_CHUNK, TT), F32)
        for k in range(TOP_K):
            p = jnp.where(rows == lp[k:k + 1, :], 1.0, p)
        xl = jnp.dot(p.astype(MXU_DTYPE), xb, preferred_element_type=F32)
        xloc[rc * SORT_CHUNK:(rc + 1) * SORT_CHUNK, :] = xl.astype(xloc.dtype)
    _move_runs(tab_ref, xloc, xs_ref, sem, True)

    @pl.when(pl.program_id(0) == pl.num_programs(0) - 1)
    def _zero_tails():
        zbuf[...] = jnp.zeros_like(zbuf)

        def per_expert(e, tot):
            n = tail_ref[0, 0, N_EXPERTS + e]
            start = tail_ref[0, 0, e]

            def per_granule(j, c):
                _run_copy(zbuf, xs_ref, sem, 0, start + j * BF16_ROWS, True).start()
                return c

            lax.fori_loop(0, n, per_granule, 0)
            return tot + n

        total = lax.fori_loop(0, N_EXPERTS, per_expert, 0)

        def wait_one(j, c):
            _run_copy(zbuf, xs_ref, sem, 0, 0, True).wait()
            return c

        lax.fori_loop(0, total, wait_one, 0)


def _dispatch(h1b, lpos, tab, tail, n_rows, *, interpret=False):
    N = h1b.shape[0]
    TT = TOK_TILE
    nt = N // TT
    return pl.pallas_call(
        functools.partial(_dispatch_kernel, tile=TT), grid=(nt,),
        in_specs=[
            pl.BlockSpec((1, 1, 2 * N_EXPERTS), lambda i: (i, 0, 0), memory_space=pltpu.SMEM),
            pl.BlockSpec((1, 1, 2 * N_EXPERTS), lambda i: (0, 0, 0), memory_space=pltpu.SMEM),
            pl.BlockSpec((TT, D_MODEL), lambda i: (i, 0)),
            pl.BlockSpec((1, TOP_K, TT), lambda i: (i, 0, 0)),
        ],
        out_specs=pl.BlockSpec(memory_space=pl.ANY),
        out_shape=jax.ShapeDtypeStruct((n_rows, D_MODEL), MXU_DTYPE),
        scratch_shapes=[pltpu.VMEM((_local_rows(TT), D_MODEL), MXU_DTYPE),
                        pltpu.VMEM((BF16_ROWS, D_MODEL), MXU_DTYPE),
                        pltpu.SemaphoreType.DMA(())],
        name="dispatch",
        compiler_params=pltpu.CompilerParams(
            dimension_semantics=("arbitrary",), vmem_limit_bytes=VMEM_LIMIT_BYTES),
        interpret=interpret,
    )(tab, tail, h1b, lpos)


def _expert_kernel(blk_e_ref, nact_ref, x_ref, wgu_ref, wd_ref, y_ref):
    j = pl.program_id(0)

    @pl.when(j < nact_ref[0])
    def _compute():
        gu = jnp.dot(x_ref[...], wgu_ref[0], preferred_element_type=F32)
        g = gu[:, :EXPERT_FF]
        h = (g * _sigmoid(g)) * gu[:, EXPERT_FF:]
        y_ref[...] = _dot(h, wd_ref[0]).astype(y_ref.dtype)

    @pl.when(j >= nact_ref[0])
    def _idle():
        y_ref[...] = jnp.zeros_like(y_ref)


def _experts(xs, wgu, wd, blk_e, nact, row_block, *, interpret=False):
    BM = row_block
    nblk = xs.shape[0] // BM
    grid_spec = pltpu.PrefetchScalarGridSpec(
        num_scalar_prefetch=2, grid=(nblk,),
        in_specs=[
            pl.BlockSpec((BM, D_MODEL), lambda j, be, na: (jnp.minimum(j, na[0] - 1), 0)),
            pl.BlockSpec((1, D_MODEL, 2 * EXPERT_FF), lambda j, be, na: (be[j], 0, 0)),
            pl.BlockSpec((1, EXPERT_FF, D_MODEL), lambda j, be, na: (be[j], 0, 0)),
        ],
        out_specs=pl.BlockSpec((BM, D_MODEL), lambda j, be, na: (jnp.where(j < na[0], j, nblk), 0)),
    )
    return pl.pallas_call(
        _expert_kernel, grid_spec=grid_spec,
        out_shape=jax.ShapeDtypeStruct(((nblk + 1) * BM, D_MODEL), MXU_DTYPE), name="experts",
        compiler_params=pltpu.CompilerParams(
            dimension_semantics=("arbitrary",), vmem_limit_bytes=VMEM_LIMIT_BYTES),
        interpret=interpret,
    )(blk_e, nact, xs, wgu, wd)


def _combine_kernel(tab_ref, lpos_ref, gate_ref, base_ref, ln2g_ref, ln2b_ref, ys_ref, out_ref, yloc, wbuf, sem,
                    *, tile):
    TT = tile
    LR = _local_rows(TT)

    @pl.when(pl.program_id(0) == 0)
    def _init():
        yloc[...] = jnp.zeros_like(yloc)

    in_flight = _start_runs(tab_ref, yloc, ys_ref, sem, False)
    lp = lpos_ref[0]
    gt = gate_ref[0]
    for rc in range(LR // SORT_CHUNK):
        rows = lax.broadcasted_iota(I32, (SORT_CHUNK, TT), 0) + rc * SORT_CHUNK
        w = jnp.zeros((SORT_CHUNK, TT), F32)
        for k in range(TOP_K):
            w = jnp.where(rows == lp[k:k + 1, :], gt[k:k + 1, :], w)
        wbuf[rc * SORT_CHUNK:(rc + 1) * SORT_CHUNK, :] = w.astype(wbuf.dtype)
    _wait_runs(in_flight, yloc, ys_ref, sem, False)
    acc = base_ref[...]
    for rc in range(LR // SORT_CHUNK):
        sl = slice(rc * SORT_CHUNK, (rc + 1) * SORT_CHUNK)
        acc = acc + _dot_tn(wbuf[sl, :], yloc[sl, :])
    out_ref[...] = _layer_norm(acc, ln2g_ref[...], ln2b_ref[...])


def _combine(ys, lpos, gate, base, ln2g, ln2b, tab, *, interpret=False):
    N = base.shape[0]
    TT = TOK_TILE
    nt = N // TT
    return pl.pallas_call(
        functools.partial(_combine_kernel, tile=TT), grid=(nt,),
        in_specs=[
            pl.BlockSpec((1, 1, 2 * N_EXPERTS), lambda i: (i, 0, 0), memory_space=pltpu.SMEM),
            pl.BlockSpec((1, TOP_K, TT), lambda i: (i, 0, 0)),
            pl.BlockSpec((1, TOP_K, TT), lambda i: (i, 0, 0)),
            pl.BlockSpec((TT, D_MODEL), lambda i: (i, 0)),
            pl.BlockSpec((1, D_MODEL), lambda i: (0, 0)),
            pl.BlockSpec((1, D_MODEL), lambda i: (0, 0)),
            pl.BlockSpec(memory_space=pl.ANY),
        ],
        out_specs=pl.BlockSpec((TT, D_MODEL), lambda i: (i, 0)),
        out_shape=jax.ShapeDtypeStruct((N, D_MODEL), F32),
        scratch_shapes=[pltpu.VMEM((_local_rows(TT), D_MODEL), MXU_DTYPE),
                        pltpu.VMEM((_local_rows(TT), TT), MXU_DTYPE),
                        pltpu.SemaphoreType.DMA(())],
        name="combine",
        compiler_params=pltpu.CompilerParams(
            dimension_semantics=("arbitrary",), vmem_limit_bytes=VMEM_LIMIT_BYTES),
        interpret=interpret,
    )(tab, lpos, gate, base, ln2g, ln2b, ys)


def _moe_weights(exp_w_gate, exp_w_up, exp_w_down, ln2_g, ln2_b):
    return (jnp.concatenate([exp_w_gate, exp_w_up], axis=-1).astype(MXU_DTYPE), exp_w_down.astype(MXU_DTYPE),
            ln2_g.reshape(1, D_MODEL), ln2_b.reshape(1, D_MODEL))


def _row_block(n_tokens):
    rb = ROW_BLOCK
    while rb > LANES and n_tokens * TOP_K // N_EXPERTS < rb:
        rb //= 2
    return rb


def _sorted_layout(pcnt, rb):
    nt = pcnt.shape[0]
    run_len = jnp.sum(pcnt, axis=0)
    region = (run_len + rb - 1) // rb * rb
    region_end = jnp.cumsum(region)
    region_start = region_end - region
    gbase = region_start[None, :] + jnp.cumsum(pcnt, axis=0) - pcnt
    tab = jnp.concatenate([pcnt, gbase], axis=1).reshape(nt, 1, 2 * N_EXPERTS).astype(I32)
    tail = jnp.concatenate([region_start + run_len, (region - run_len) // BF16_ROWS]
                           ).reshape(1, 1, 2 * N_EXPERTS).astype(I32)
    max_rows = nt * (TOK_TILE * TOP_K + N_EXPERTS * (BF16_ROWS - 1)) + N_EXPERTS * (rb - BF16_ROWS)
    nblk = -(-max_rows // rb)
    blk_row = jnp.arange(nblk, dtype=I32) * rb
    blk_e = jnp.minimum(jnp.sum((region_end[None, :] <= blk_row[:, None]).astype(I32), axis=1), N_EXPERTS - 1)
    nact = (region_end[-1] // rb).reshape(1).astype(I32)
    return tab, tail, blk_e, nact, nblk * rb


def _layer(x, gbuf0, s0, lbuf0, h0, mix_w, post_w, moe_w, *, chunk, reset_first, interpret=False):
    B, T, D = x.shape
    mix_in, gbuf, s_new, lbuf, h_new = _mixer(x, gbuf0, s0, lbuf0, h0, mix_w, chunk=chunk,
                                              reset_first=reset_first, interpret=interpret)
    N = B * T
    h1b, base, lpos, gate, pcnt = _post(mix_in.reshape(N, -1), x.reshape(N, D), post_w, interpret=interpret)
    rb = _row_block(N)
    tab, tail, blk_e, nact, n_rows = _sorted_layout(pcnt[:, :, 0], rb)
    wgu, wd, ln2g, ln2b = moe_w
    xs = _dispatch(h1b, lpos, tab, tail, n_rows, interpret=interpret)
    ys = _experts(xs, wgu, wd, blk_e, nact, rb, interpret=interpret)
    y = _combine(ys, lpos, gate, base, ln2g, ln2b, tab, interpret=interpret)
    keep = CONV_PAD - (CONV_W - 1)
    states = (gbuf[:, keep:], s_new, lbuf[:, keep:], h_new.reshape(B, D))
    return y.reshape(B, T, D), states


def kernel(x_prompt, x_sample, state_gdn_conv, state_gdn, state_lru_conv, state_lru, w_in, gdn_conv_w, gdn_a_log, gdn_dt_bias, gdn_norm_w, lru_conv_w, lru_conv_b, lru_wa, lru_ba, lru_wx, lru_bx, lru_lambda, w_out, ln1_g, ln1_b, router_w, router_bias, exp_w_gate, exp_w_up, exp_w_down, sh_w_gate, sh_w_up, sh_w_down, ln2_g, ln2_b):
    mix_w = _mixer_weights(w_in[0], gdn_conv_w[0], gdn_a_log[0], gdn_dt_bias[0], gdn_norm_w[0], lru_conv_w[0],
                           lru_conv_b[0], lru_wa[0], lru_ba[0], lru_wx[0], lru_bx[0], lru_lambda[0])
    post_w = _post_weights(w_out[0], ln1_g[0], ln1_b[0], router_w[0], router_bias[0],
                           sh_w_gate[0], sh_w_up[0], sh_w_down[0])
    moe_w = _moe_weights(exp_w_gate[0], exp_w_up[0], exp_w_down[0], ln2_g[0], ln2_b[0])
    bp, bs = x_prompt.shape[0], x_sample.shape[0]
    zeros = lambda *s: jnp.zeros(s, F32)
    pad_rows = lambda a: jnp.pad(a, ((0, 0), (CONV_PAD - (CONV_W - 1), 0), (0, 0)))
    yp, sp = _layer(x_prompt, zeros(bp, CONV_PAD, QKV_GROUPS * LANES), zeros(bp, HEADS, HEAD_DIM, HEAD_DIM),
                    zeros(bp, CONV_PAD, D_MODEL), zeros(bp, LRU_BLOCKS, 1, LANES),
                    mix_w, post_w, moe_w, chunk=64, reset_first=True)
    ys, ss = _layer(x_sample, pad_rows(state_gdn_conv[0]), state_gdn[0], pad_rows(state_lru_conv[0]),
                    state_lru[0].reshape(bs, LRU_BLOCKS, 1, LANES),
                    mix_w, post_w, moe_w, chunk=x_sample.shape[1], reset_first=False)
    return (yp, ys) + tuple(a[None] for a in sp) + tuple(a[None] for a in ss)
```

```python
import functools
import math

import jax
import jax.numpy as jnp
from jax import lax
from jax.experimental import pallas as pl
from jax.experimental.pallas import tpu as pltpu

F32 = jnp.float32
I32 = jnp.int32
MXU_DTYPE = jnp.bfloat16

D_MODEL = 1024
HEADS = 8
HEAD_DIM = 128
QKV_GROUPS = 3 * HEADS
LRU_BLOCKS = 8
CONV_W = 4
CONV_PAD = 8
N_EXPERTS = 64
N_GROUPS = 8
GROUP_SIZE = N_EXPERTS // N_GROUPS
TOPK_GROUPS = 4
TOP_K = 8
EXPERT_FF = 256
SHARED_FF = 256
ROUTED_SCALE = 2.5
LRU_C = 8.0
DEPTH = 1
DN_ALPHA = (2.0 * DEPTH) ** 0.25
LN_EPS = 1e-5
RMS_EPS = 1e-6
L2_EPS = 1e-6

LANES = 128
SUBLANES = 8
BF16_ROWS = 16
VMEM_LIMIT_BYTES = 56 * 1024 * 1024

MIX_TILE = 256
HEAD_GROUP = 8
TOK_TILE = 256
ROW_BLOCK = 512


def _dot(a, b):
    return jnp.dot(a.astype(MXU_DTYPE), b.astype(MXU_DTYPE), preferred_element_type=F32)


def _dot_nt(a, b):
    return lax.dot_general(a.astype(MXU_DTYPE), b.astype(MXU_DTYPE),
                           (((1,), (1,)), ((), ())), preferred_element_type=F32)


def _dot_tn(a, b):
    return lax.dot_general(a.astype(MXU_DTYPE), b.astype(MXU_DTYPE),
                           (((0,), (0,)), ((), ())), preferred_element_type=F32)


def _dot_f32(a, b):
    return jnp.dot(a, b, precision=lax.Precision.HIGHEST, preferred_element_type=F32)


def _sigmoid(x):
    return 1.0 / (1.0 + jnp.exp(-x))


def _softplus(x):
    return jnp.maximum(x, 0.0) + jnp.log1p(jnp.exp(-jnp.abs(x)))


def _widen(col, width):
    if width <= LANES:
        return col[:, :width]
    return jnp.concatenate([col] * (width // LANES), axis=1)


def _mixer_kernel(x_ref, gbuf0_ref, s0_ref, lbuf0_ref, h0_ref,
                  wmain_ref, wba_ref, gcw_ref, alog_ref, dtb_ref, gnw_ref,
                  lcw_ref, lcb_ref, wax_ref, lbax_ref, lam_ref,
                  mix_ref, gbuf_ref, s_ref, lbuf_ref, h_ref,
                  qkv_s, z_s, lx_s, ly_s, col_s, row_s, o_s, hst_s, lsig_s,
                  *, tile, chunk, reset_first):
    TT, C = tile, chunk
    NC = TT // C
    t = pl.program_id(1)

    @pl.when(t == 0)
    def _load_state():
        for g in range(QKV_GROUPS):
            qkv_s[g, 0:CONV_PAD, :] = gbuf0_ref[0, :, g * LANES:(g + 1) * LANES]
        for k in range(LRU_BLOCKS):
            lx_s[k, 0:CONV_PAD, :] = lbuf0_ref[0, :, k * LANES:(k + 1) * LANES]
        s_ref[...] = s0_ref[...]
        hst_s[...] = h0_ref[0]

    xb = x_ref[0].astype(MXU_DTYPE)
    for gp in range(QKV_GROUPS // 2):
        res = jnp.dot(xb, wmain_ref[:, gp * 256:(gp + 1) * 256], preferred_element_type=F32)
        qkv_s[2 * gp, CONV_PAD:CONV_PAD + TT, :] = res[:, :LANES]
        qkv_s[2 * gp + 1, CONV_PAD:CONV_PAD + TT, :] = res[:, LANES:]
    col0 = QKV_GROUPS * LANES
    for gp in range(HEADS // 2):
        res = jnp.dot(xb, wmain_ref[:, col0 + gp * 256:col0 + (gp + 1) * 256], preferred_element_type=F32)
        z_s[2 * gp] = res[:, :LANES]
        z_s[2 * gp + 1] = res[:, LANES:]
    col0 += HEADS * LANES
    for gp in range(LRU_BLOCKS // 2):
        res = jnp.dot(xb, wmain_ref[:, col0 + gp * 256:col0 + (gp + 1) * 256], preferred_element_type=F32)
        lx_s[2 * gp, CONV_PAD:CONV_PAD + TT, :] = res[:, :LANES]
        lx_s[2 * gp + 1, CONV_PAD:CONV_PAD + TT, :] = res[:, LANES:]
    col0 += LRU_BLOCKS * LANES
    for gp in range(LRU_BLOCKS // 2):
        res = jnp.dot(xb, wmain_ref[:, col0 + gp * 256:col0 + (gp + 1) * 256], preferred_element_type=F32)
        ly_s[2 * gp] = res[:, :LANES]
        ly_s[2 * gp + 1] = res[:, LANES:]
    ba = jnp.dot(xb, wba_ref[...], preferred_element_type=F32)

    for g in range(QKV_GROUPS):
        gbuf_ref[0, :, g * LANES:(g + 1) * LANES] = qkv_s[g, TT:TT + CONV_PAD, :]
    for k in range(LRU_BLOCKS):
        lbuf_ref[0, :, k * LANES:(k + 1) * LANES] = lx_s[k, TT:TT + CONV_PAD, :]

    row = lax.broadcasted_iota(I32, (TT, TT), 0)
    colm = lax.broadcasted_iota(I32, (TT, TT), 1)
    same_chunk = (row // C) == (colm // C)
    causal = same_chunk & (row >= colm)
    strict = same_chunk & (row > colm)
    beta_all = _sigmoid(ba)
    g_all = -jnp.exp(alog_ref[...]) * _softplus(ba + dtb_ref[...])
    gc_all = _dot_f32(causal.astype(F32), g_all)
    gt_all = _dot_f32(same_chunk.astype(F32), g_all)
    gc_t = gc_all.T
    for h in range(HEADS):
        col_s[0, h] = jnp.broadcast_to(beta_all[:, h:h + 1], (TT, LANES))
        col_s[1, h] = jnp.broadcast_to(gc_all[:, HEADS + h:HEADS + h + 1], (TT, LANES))
        col_s[2, h] = jnp.broadcast_to(gt_all[:, HEADS + h:HEADS + h + 1], (TT, LANES))
        row_s[h] = gc_t[HEADS + h:HEADS + h + 1, :]

    def conv_silu(g):
        w = gcw_ref[g]
        acc = qkv_s[g, pl.ds(CONV_PAD - 3, TT), :] * w[0:1]
        acc = acc + qkv_s[g, pl.ds(CONV_PAD - 2, TT), :] * w[1:2]
        acc = acc + qkv_s[g, pl.ds(CONV_PAD - 1, TT), :] * w[2:3]
        acc = acc + qkv_s[g, pl.ds(CONV_PAD, TT), :] * w[3:4]
        return acc * _sigmoid(acc)

    n_levels = int(math.log2(C))
    cat = lambda parts, axis: parts[0] if len(parts) == 1 else jnp.concatenate(parts, axis=axis)
    G = TT // SUBLANES
    sub = lax.broadcasted_iota(I32, (G, SUBLANES, LANES), 1)
    first_row = lax.broadcasted_iota(I32, (TT, LANES), 0) == 0

    def lru_block(kb_):
        w = lcw_ref[kb_]
        xc = lx_s[kb_, pl.ds(CONV_PAD - 3, TT), :] * w[0:1]
        xc = xc + lx_s[kb_, pl.ds(CONV_PAD - 2, TT), :] * w[1:2]
        xc = xc + lx_s[kb_, pl.ds(CONV_PAD - 1, TT), :] * w[2:3]
        xc = xc + lx_s[kb_, pl.ds(CONV_PAD, TT), :] * w[3:4]
        xc = xc + lcb_ref[kb_]
        gates = _sigmoid(_dot(xc, wax_ref[kb_]) + lbax_ref[kb_])
        r = gates[:, :LANES]
        gi = gates[:, LANES:]
        log_a = (LRU_C * r) * lsig_s[kb_]
        a = jnp.exp(log_a)
        th = jnp.tanh(log_a)
        mult = jnp.sqrt(-2.0 * th / (1.0 - th))
        if reset_first:
            mult = jnp.where(first_row & (t == 0), 1.0, mult)
        hprev = hst_s[kb_]
        b = mult * gi * xc + jnp.where(first_row, a * hprev, 0.0)
        a3 = a.reshape(G, SUBLANES, LANES)
        b3 = b.reshape(G, SUBLANES, LANES)
        for s in (1, 2, 4):
            a_sh = jnp.where(sub >= s, pltpu.roll(a3, s, 1), 1.0)
            b_sh = jnp.where(sub >= s, pltpu.roll(b3, s, 1), 0.0)
            b3 = a3 * b_sh + b3
            a3 = a3 * a_sh
        rows = []
        hc = jnp.zeros((1, LANES), F32)
        for gidx in range(G):
            hgrp = b3[gidx] if gidx == 0 else a3[gidx] * hc + b3[gidx]
            hc = hgrp[SUBLANES - 1:SUBLANES, :]
            rows.append(hgrp)
        hst_s[kb_] = hc
        o_s[HEADS + kb_] = jnp.concatenate(rows, axis=0) * jax.nn.gelu(ly_s[kb_])

    lsig_s[...] = -_softplus(-lam_ref[...])
    n_groups = HEADS // HEAD_GROUP
    lru_per_group = LRU_BLOCKS // n_groups

    def head_group_body(hg, carry):
        hs = [hg * HEAD_GROUP + i for i in range(HEAD_GROUP)]
        R = range(HEAD_GROUP)
        lru_todo = [hg * lru_per_group + i for i in range(lru_per_group)]

        def lru_step():
            if lru_todo:
                lru_block(lru_todo.pop(0))

        q = [conv_silu(h) for h in hs]
        k = [conv_silu(HEADS + h) for h in hs]
        v = [conv_silu(2 * HEADS + h) for h in hs]
        q = [x * lax.rsqrt(jnp.sum(x * x, axis=-1, keepdims=True) + L2_EPS) * (HEAD_DIM ** -0.5) for x in q]
        k = [x * lax.rsqrt(jnp.sum(x * x, axis=-1, keepdims=True) + L2_EPS) for x in k]
        beta = [col_s[0, h] for h in hs]
        gcol = [col_s[1, h] for h in hs]
        gtot = [col_s[2, h] for h in hs]
        decay = [jnp.where(causal, jnp.exp(_widen(gcol[i], TT) - row_s[hs[i]]), 0.0) for i in R]
        eg = [jnp.exp(g) for g in gcol]
        kb = [k[i] * beta[i] for i in R]
        qks = [_dot_nt(jnp.concatenate([kb[i], q[i]], axis=0), k[i]) for i in R]
        nmat = [jnp.where(strict, -(qks[i][:TT] * decay[i]), 0.0) for i in R]
        qk = [qks[i][TT:] * decay[i] for i in R]
        lru_step()
        tm = nmat
        if n_levels > 1:
            npow = [_dot(n, n) for n in nmat]
            for j in range(1, n_levels):
                if j < n_levels - 1:
                    r2 = [_dot(jnp.concatenate([tm[i], npow[i]], axis=0), npow[i]) for i in R]
                    tm = [tm[i] + npow[i] + r2[i][:TT] for i in R]
                    npow = [r2[i][TT:] for i in R]
                else:
                    tm = [tm[i] + npow[i] + _dot(tm[i], npow[i]) for i in R]
                lru_step()
        rhs =[jnp.concatenate([kb[i] * eg[i], v[i] * beta[i]], axis=1) for i in R]
        wu = [rhs[i] + _dot(tm[i], rhs[i]) for i in R]
        qd = [q[i] * eg[i] for i in R]
        kd = [k[i] * jnp.exp(gtot[i] - gcol[i]) for i in R]
        egt = [jnp.exp(g) for g in gtot]
        kwu = [[_dot_tn(kd[i][c * C:(c + 1) * C], wu[i][c * C:(c + 1) * C]) for c in range(NC)] for i in R]
        state = [s_ref[0, h] for h in hs]
        starts = [[] for _ in R]
        for c in range(NC):
            for i in R:
                starts[i].append(state[i])
                m = kwu[i][c]
                state[i] = (state[i] * egt[i][c * C:c * C + 1, :] + m[:, LANES:]) - _dot(m[:, :LANES], state[i])
        for i in R:
            s_ref[0, hs[i]] = state[i]
        while lru_todo:
            lru_step()
        for i in R:
            wq =[_dot(jnp.concatenate([wu[i][c * C:(c + 1) * C, :LANES], qd[i][c * C:(c + 1) * C]], axis=0),
                       starts[i][c]) for c in range(NC)]
            vn = cat([wu[i][c * C:(c + 1) * C, LANES:] - wq[c][:C] for c in range(NC)], 0)
            o = cat([wq[c][C:] for c in range(NC)], 0) + _dot(qk[i], vn)
            o = o * lax.rsqrt(jnp.mean(o * o, axis=-1, keepdims=True) + RMS_EPS) * gnw_ref[...]
            zz = z_s[hs[i]]
            o_s[hs[i]] = o * (zz * _sigmoid(zz))
        return carry

    lax.fori_loop(0, HEADS // HEAD_GROUP, head_group_body, 0)

    for g in range(HEADS + LRU_BLOCKS):
        mix_ref[0, :, g * LANES:(g + 1) * LANES] = o_s[g].astype(mix_ref.dtype)
    h_ref[0] = hst_s[...]
    for g in range(QKV_GROUPS):
        qkv_s[g, 0:CONV_PAD, :] = qkv_s[g, TT:TT + CONV_PAD, :]
    for k in range(LRU_BLOCKS):
        lx_s[k, 0:CONV_PAD, :] = lx_s[k, TT:TT + CONV_PAD, :]


def _const_spec(shape):
    nd = len(shape)
    return pl.BlockSpec(shape, lambda b, t, _n=nd: (0,) * _n)


def _mixer(x, gbuf0, s0, lbuf0, h0, wts, *, chunk, reset_first, interpret=False):
    B, T, _ = x.shape
    TT = min(MIX_TILE, T)
    assert T % TT == 0 and TT % chunk == 0 and TT % SUBLANES == 0
    NT = T // TT
    kern = functools.partial(_mixer_kernel, tile=TT, chunk=chunk, reset_first=reset_first)
    in_specs = [
        pl.BlockSpec((1, TT, D_MODEL), lambda b, t: (b, t, 0)),
        pl.BlockSpec((1, CONV_PAD, QKV_GROUPS * LANES), lambda b, t: (b, 0, 0)),
        pl.BlockSpec((1, HEADS, HEAD_DIM, HEAD_DIM), lambda b, t: (b, 0, 0, 0)),
        pl.BlockSpec((1, CONV_PAD, LRU_BLOCKS * LANES), lambda b, t: (b, 0, 0)),
        pl.BlockSpec((1, LRU_BLOCKS, 1, LANES), lambda b, t: (b, 0, 0, 0)),
    ] + [_const_spec(w.shape) for w in wts]
    out_shape = (
        jax.ShapeDtypeStruct((B, T, (HEADS + LRU_BLOCKS) * LANES), MXU_DTYPE),
        jax.ShapeDtypeStruct((B, CONV_PAD, QKV_GROUPS * LANES), F32),
        jax.ShapeDtypeStruct((B, HEADS, HEAD_DIM, HEAD_DIM), F32),
        jax.ShapeDtypeStruct((B, CONV_PAD, LRU_BLOCKS * LANES), F32),
        jax.ShapeDtypeStruct((B, LRU_BLOCKS, 1, LANES), F32),
    )
    out_specs = (
        pl.BlockSpec((1, TT, (HEADS + LRU_BLOCKS) * LANES), lambda b, t: (b, t, 0)),
        pl.BlockSpec((1, CONV_PAD, QKV_GROUPS * LANES), lambda b, t: (b, 0, 0)),
        pl.BlockSpec((1, HEADS, HEAD_DIM, HEAD_DIM), lambda b, t: (b, 0, 0, 0)),
        pl.BlockSpec((1, CONV_PAD, LRU_BLOCKS * LANES), lambda b, t: (b, 0, 0)),
        pl.BlockSpec((1, LRU_BLOCKS, 1, LANES), lambda b, t: (b, 0, 0, 0)),
    )
    scratch = [
        pltpu.VMEM((QKV_GROUPS, TT + CONV_PAD, LANES), F32),
        pltpu.VMEM((HEADS, TT, LANES), F32),
        pltpu.VMEM((LRU_BLOCKS, TT + CONV_PAD, LANES), F32),
        pltpu.VMEM((LRU_BLOCKS, TT, LANES), F32),
        pltpu.VMEM((3, HEADS, TT, LANES), F32),
        pltpu.VMEM((HEADS, 1, TT), F32),
        pltpu.VMEM((HEADS + LRU_BLOCKS, TT, LANES), F32),
        pltpu.VMEM((LRU_BLOCKS, 1, LANES), F32),
        pltpu.VMEM((LRU_BLOCKS, 1, LANES), F32),
    ]
    return pl.pallas_call(
        kern, grid=(B, NT), in_specs=in_specs, out_specs=out_specs, out_shape=out_shape,
        scratch_shapes=scratch, name="mixer",
        compiler_params=pltpu.CompilerParams(
            dimension_semantics=("parallel", "arbitrary"), vmem_limit_bytes=VMEM_LIMIT_BYTES),
        interpret=interpret,
    )(x, gbuf0, s0, lbuf0, h0, *wts)


def _mixer_weights(w_in, gdn_conv_w, gdn_a_log, gdn_dt_bias, gdn_norm_w,
                   lru_conv_w, lru_conv_b, lru_wa, lru_ba, lru_wx, lru_bx, lru_lambda):
    qkvz = QKV_GROUPS * LANES + HEADS * LANES
    ba0 = qkvz
    lx0 = qkvz + 2 * HEADS
    wmain = jnp.concatenate([w_in[:, :qkvz], w_in[:, lx0:]], axis=1).astype(MXU_DTYPE)
    wba = jnp.pad(w_in[:, ba0:lx0], ((0, 0), (0, LANES - 2 * HEADS))).astype(MXU_DTYPE)
    gcw = gdn_conv_w.reshape(CONV_W, QKV_GROUPS, LANES).transpose(1, 0, 2)
    pad8 = (HEADS, LANES - 2 * HEADS)
    alog = jnp.pad(gdn_a_log, pad8).reshape(1, LANES)
    dtb = jnp.pad(gdn_dt_bias, pad8).reshape(1, LANES)
    gnw = gdn_norm_w.reshape(1, LANES)
    lcw = lru_conv_w.reshape(CONV_W, LRU_BLOCKS, LANES).transpose(1, 0, 2)
    lcb = lru_conv_b.reshape(LRU_BLOCKS, 1, LANES)
    wax = jnp.concatenate([lru_wa, lru_wx], axis=-1).astype(MXU_DTYPE)
    lbax = jnp.concatenate([lru_ba.reshape(LRU_BLOCKS, 1, LANES),
                            lru_bx.reshape(LRU_BLOCKS, 1, LANES)], axis=-1)
    lam = lru_lambda.reshape(LRU_BLOCKS, 1, LANES)
    return (wmain, wba, gcw, alog, dtb, gnw, lcw, lcb, wax, lbax, lam)


def _layer_norm(x, g, b):
    mu = jnp.mean(x, axis=-1, keepdims=True)
    xc = x - mu
    var = jnp.mean(xc * xc, axis=-1, keepdims=True)
    return xc * lax.rsqrt(var + LN_EPS) * g + b


def _max01(x):
    return jnp.max(jnp.max(x, axis=0, keepdims=True), axis=1, keepdims=True)


def _min01(x):
    return jnp.min(jnp.min(x, axis=0, keepdims=True), axis=1, keepdims=True)


def _sum01(x):
    return jnp.sum(jnp.sum(x, axis=0, keepdims=True), axis=1, keepdims=True)


def _post_kernel(mix_ref, x_ref, wout_ref, ln1g_ref, ln1b_ref, rwt_ref, rbias_ref, shgu_ref, shd_ref,
                 h1b_ref, base_ref, lpos_ref, gate_ref, pcnt_ref, *, tile):
    TT = tile
    hpre = DN_ALPHA * x_ref[...] + jnp.dot(mix_ref[...], wout_ref[...], preferred_element_type=F32)
    h1 = _layer_norm(hpre, ln1g_ref[...], ln1b_ref[...])
    h1b = h1.astype(MXU_DTYPE)
    h1b_ref[...] = h1b

    gu = jnp.dot(h1b, shgu_ref[...], preferred_element_type=F32)
    g_sh = gu[:, :SHARED_FF]
    hsh = (g_sh * _sigmoid(g_sh)) * gu[:, SHARED_FF:]
    base_ref[...] = DN_ALPHA * h1 + _dot(hsh, shd_ref[...])

    logits = lax.dot_general(rwt_ref[...], h1b, (((1,), (1,)), ((), ())), preferred_element_type=F32)
    scores = _sigmoid(logits)
    s3 = scores.reshape(N_GROUPS, GROUP_SIZE, TT)
    b3 = (scores + rbias_ref[...]).reshape(N_GROUPS, GROUP_SIZE, TT)
    member = lax.broadcasted_iota(I32, (N_GROUPS, GROUP_SIZE, TT), 1)
    group = lax.broadcasted_iota(I32, (N_GROUPS, GROUP_SIZE, TT), 0)
    expert = group * GROUP_SIZE + member
    neg_inf = jnp.float32(-jnp.inf)
    m1 = jnp.max(b3, axis=1, keepdims=True)
    first1 = jnp.min(jnp.where(b3 == m1, member, GROUP_SIZE), axis=1, keepdims=True)
    m2 = jnp.max(jnp.where(member == first1, neg_inf, b3), axis=1, keepdims=True)
    gs = m1 + m2
    gidx = lax.broadcasted_iota(I32, (N_GROUPS, 1, TT), 0)
    grank = jnp.zeros((N_GROUPS, 1, TT), I32)
    for go in range(N_GROUPS):
        other = gs[go:go + 1]
        beats = (other > gs) | ((other == gs) & (go < gidx))
        grank = grank + beats.astype(I32)
    masked = jnp.where(grank < TOPK_GROUPS, b3, neg_inf)
    onehots, gates = [], []
    for _ in range(TOP_K):
        mx = _max01(masked)
        first = _min01(jnp.where(masked == mx, expert, N_EXPERTS))
        hit = expert == first
        onehots.append(hit)
        gates.append(_sum01(jnp.where(hit, s3, 0.0)))
        masked = jnp.where(hit, neg_inf, masked)
    gsum = gates[0]
    for gk in gates[1:]:
        gsum = gsum + gk
    sel3 = jnp.zeros((N_GROUPS, GROUP_SIZE, TT), F32)
    for hit in onehots:
        sel3 = jnp.where(hit, 1.0, sel3)
    sel = sel3.reshape(N_EXPERTS, TT)
    tr = lax.broadcasted_iota(I32, (TT, TT), 0)
    tc = lax.broadcasted_iota(I32, (TT, TT), 1)
    rank = _dot(sel, (tr < tc).astype(F32))
    cnt = jnp.sum(sel, axis=1, keepdims=True)
    pcnt = jnp.floor((cnt + (BF16_ROWS - 1)) * (1.0 / BF16_ROWS)) * BF16_ROWS
    er = lax.broadcasted_iota(I32, (N_EXPERTS, N_EXPERTS), 0)
    ec = lax.broadcasted_iota(I32, (N_EXPERTS, N_EXPERTS), 1)
    pcnt_b = jnp.broadcast_to(pcnt, (N_EXPERTS, LANES))
    loff = _dot_f32((ec < er).astype(F32), pcnt_b)[:, 0:1]
    lpos3 = (loff + rank).reshape(N_GROUPS, GROUP_SIZE, TT)
    for k in range(TOP_K):
        lp = _sum01(jnp.where(onehots[k], lpos3, 0.0))
        lpos_ref[0, k:k + 1, :] = lp.reshape(1, TT).astype(I32)
        gate_ref[0, k:k + 1, :] = (gates[k] / gsum * ROUTED_SCALE).reshape(1, TT)
    pcnt_ref[0] = pcnt_b.astype(I32)


def _post(mix_in, x2, wts, *, interpret=False):
    N = x2.shape[0]
    TT = TOK_TILE
    assert N % TT == 0
    nt = N // TT
    tok = lambda w: pl.BlockSpec((TT, w), lambda i: (i, 0))
    const = lambda a: pl.BlockSpec(a.shape, lambda i, _n=a.ndim: (0,) * _n)
    out_shape = (
        jax.ShapeDtypeStruct((N, D_MODEL), MXU_DTYPE),
        jax.ShapeDtypeStruct((N, D_MODEL), F32),
        jax.ShapeDtypeStruct((nt, TOP_K, TT), I32),
        jax.ShapeDtypeStruct((nt, TOP_K, TT), F32),
        jax.ShapeDtypeStruct((nt, N_EXPERTS, LANES), I32),
    )
    out_specs = (
        tok(D_MODEL), tok(D_MODEL),
        pl.BlockSpec((1, TOP_K, TT), lambda i: (i, 0, 0)),
        pl.BlockSpec((1, TOP_K, TT), lambda i: (i, 0, 0)),
        pl.BlockSpec((1, N_EXPERTS, LANES), lambda i: (i, 0, 0)),
    )
    return pl.pallas_call(
        functools.partial(_post_kernel, tile=TT), grid=(nt,),
        in_specs=[tok(mix_in.shape[1]), tok(D_MODEL)] + [const(w) for w in wts],
        out_specs=out_specs, out_shape=out_shape, name="post",
        compiler_params=pltpu.CompilerParams(
            dimension_semantics=("parallel",), vmem_limit_bytes=VMEM_LIMIT_BYTES),
        interpret=interpret,
    )(mix_in, x2, *wts)


def _post_weights(w_out, ln1_g, ln1_b, router_w, router_bias, sh_w_gate, sh_w_up, sh_w_down):
    return (w_out.astype(MXU_DTYPE), ln1_g.reshape(1, D_MODEL), ln1_b.reshape(1, D_MODEL),
            router_w.T.astype(MXU_DTYPE), router_bias.reshape(N_EXPERTS, 1),
            jnp.concatenate([sh_w_gate, sh_w_up], axis=1).astype(MXU_DTYPE), sh_w_down.astype(MXU_DTYPE))


SORT_CHUNK = 512


def _local_rows(tile):
    worst = tile * TOP_K + N_EXPERTS * (BF16_ROWS - 1)
    return -(-worst // SORT_CHUNK) * SORT_CHUNK


def _run_copy(loc_ref, hbm_ref, sem, loc_row, hbm_row, to_hbm):
    loc = loc_ref.at[pl.ds(pl.multiple_of(loc_row, BF16_ROWS), BF16_ROWS), :]
    hbm = hbm_ref.at[pl.ds(pl.multiple_of(hbm_row, BF16_ROWS), BF16_ROWS), :]
    return pltpu.make_async_copy(loc, hbm, sem) if to_hbm else pltpu.make_async_copy(hbm, loc, sem)


def _move_runs(tab_ref, loc_ref, hbm_ref, sem, to_hbm):
    def per_expert(e, off):
        n = tab_ref[0, 0, e] // BF16_ROWS
        dst0 = tab_ref[0, 0, N_EXPERTS + e]

        def per_pair(i, c):
            j = 2 * i
            _run_copy(loc_ref, hbm_ref, sem, off + j * BF16_ROWS, dst0 + j * BF16_ROWS, to_hbm).start(priority=0)

            @pl.when(j + 1 < n)
            def _odd():
                _run_copy(loc_ref, hbm_ref, sem, off + (j + 1) * BF16_ROWS, dst0 + (j + 1) * BF16_ROWS,
                          to_hbm).start(priority=1)
            return c

        lax.fori_loop(0, (n + 1) // 2, per_pair, 0)
        return off + n * BF16_ROWS

    total = lax.fori_loop(0, N_EXPERTS, per_expert, 0)

    def wait_one(j, c):
        _run_copy(loc_ref, hbm_ref, sem, 0, 0, to_hbm).wait()
        return c

    lax.fori_loop(0, total // BF16_ROWS, wait_one, 0)


def _dispatch_kernel(tab_ref, tail_ref, h1b_ref, lpos_ref, xs_ref, xloc, zbuf, sem, *, tile):
    TT = tile
    LR = _local_rows(TT)
    xb = h1b_ref[...]
    lp = lpos_ref[0]
    for rc in range(LR // SORT_CHUNK):
        rows = lax.broadcasted_iota(I32, (SORT_CHUNK, TT), 0) + rc * SORT_CHUNK
        p = jnp.zeros((SORT_CHUNK, TT), F32)
        for k in range(TOP_K):
            p = jnp.where(rows == lp[k:k + 1, :], 1.0, p)
        xl = jnp.dot(p.astype(MXU_DTYPE), xb, preferred_element_type=F32)
        xloc[rc * SORT_CHUNK:(rc + 1) * SORT_CHUNK, :] = xl.astype(xloc.dtype)
    _move_runs(tab_ref, xloc, xs_ref, sem, True)

    @pl.when(pl.program_id(0) == pl.num_programs(0) - 1)
    def _zero_tails():
        zbuf[...] = jnp.zeros_like(zbuf)

        def per_expert(e, tot):
            n = tail_ref[0, 0, N_EXPERTS + e]
            start = tail_ref[0, 0, e]

            def per_granule(j, c):
                _run_copy(zbuf, xs_ref, sem, 0, start + j * BF16_ROWS, True).start()
                return c

            lax.fori_loop(0, n, per_granule, 0)
            return tot + n

        total = lax.fori_loop(0, N_EXPERTS, per_expert, 0)

        def wait_one(j, c):
            _run_copy(zbuf, xs_ref, sem, 0, 0, True).wait()
            return c

        lax.fori_loop(0, total, wait_one, 0)


def _dispatch(h1b, lpos, tab, tail, n_rows, *, interpret=False):
    N = h1b.shape[0]
    TT = TOK_TILE
    nt = N // TT
    return pl.pallas_call(
        functools.partial(_dispatch_kernel, tile=TT), grid=(nt,),
        in_specs=[
            pl.BlockSpec((1, 1, 2 * N_EXPERTS), lambda i: (i, 0, 0), memory_space=pltpu.SMEM),
            pl.BlockSpec((1, 1, 2 * N_EXPERTS), lambda i: (0, 0, 0), memory_space=pltpu.SMEM),
            pl.BlockSpec((TT, D_MODEL), lambda i: (i, 0)),
            pl.BlockSpec((1, TOP_K, TT), lambda i: (i, 0, 0)),
        ],
        out_specs=pl.BlockSpec(memory_space=pl.ANY),
        out_shape=jax.ShapeDtypeStruct((n_rows, D_MODEL), MXU_DTYPE),
        scratch_shapes=[pltpu.VMEM((_local_rows(TT), D_MODEL), MXU_DTYPE),
                        pltpu.VMEM((BF16_ROWS, D_MODEL), MXU_DTYPE),
                        pltpu.SemaphoreType.DMA(())],
        name="dispatch",
        compiler_params=pltpu.CompilerParams(
            dimension_semantics=("arbitrary",), vmem_limit_bytes=VMEM_LIMIT_BYTES),
        interpret=interpret,
    )(tab, tail, h1b, lpos)


def _expert_kernel(blk_e_ref, nact_ref, x_ref, wgu_ref, wd_ref, y_ref):
    j = pl.program_id(0)

    @pl.when(j < nact_ref[0])
    def _compute():
        gu = jnp.dot(x_ref[...], wgu_ref[0], preferred_element_type=F32)
        g = gu[:, :EXPERT_FF]
        h = (g * _sigmoid(g)) * gu[:, EXPERT_FF:]
        y_ref[...] = _dot(h, wd_ref[0]).astype(y_ref.dtype)

    @pl.when(j >= nact_ref[0])
    def _idle():
        y_ref[...] = jnp.zeros_like(y_ref)


def _experts(xs, wgu, wd, blk_e, nact, row_block, *, interpret=False):
    BM = row_block
    nblk = xs.shape[0] // BM
    grid_spec = pltpu.PrefetchScalarGridSpec(
        num_scalar_prefetch=2, grid=(nblk,),
        in_specs=[
            pl.BlockSpec((BM, D_MODEL), lambda j, be, na: (jnp.minimum(j, na[0] - 1), 0)),
            pl.BlockSpec((1, D_MODEL, 2 * EXPERT_FF), lambda j, be, na: (be[j], 0, 0)),
            pl.BlockSpec((1, EXPERT_FF, D_MODEL), lambda j, be, na: (be[j], 0, 0)),
        ],
        out_specs=pl.BlockSpec((BM, D_MODEL), lambda j, be, na: (jnp.where(j < na[0], j, nblk), 0)),
    )
    return pl.pallas_call(
        _expert_kernel, grid_spec=grid_spec,
        out_shape=jax.ShapeDtypeStruct(((nblk + 1) * BM, D_MODEL), MXU_DTYPE), name="experts",
        compiler_params=pltpu.CompilerParams(
            dimension_semantics=("arbitrary",), vmem_limit_bytes=VMEM_LIMIT_BYTES),
        interpret=interpret,
    )(blk_e, nact, xs, wgu, wd)


def _combine_kernel(tab_ref, lpos_ref, gate_ref, base_ref, ln2g_ref, ln2b_ref, ys_ref, out_ref, yloc, sem,
                    *, tile):
    TT = tile
    LR = _local_rows(TT)

    @pl.when(pl.program_id(0) == 0)
    def _init():
        yloc[...] = jnp.zeros_like(yloc)

    _move_runs(tab_ref, yloc, ys_ref, sem, False)
    lp = lpos_ref[0]
    gt = gate_ref[0]
    acc = base_ref[...]
    for rc in range(LR // SORT_CHUNK):
        rows = lax.broadcasted_iota(I32, (SORT_CHUNK, TT), 0) + rc * SORT_CHUNK
        w = jnp.zeros((SORT_CHUNK, TT), F32)
        for k in range(TOP_K):
            w = jnp.where(rows == lp[k:k + 1, :], gt[k:k + 1, :], w)
        acc = acc + _dot_tn(w, yloc[rc * SORT_CHUNK:(rc + 1) * SORT_CHUNK, :])
    out_ref[...] = _layer_norm(acc, ln2g_ref[...], ln2b_ref[...])


def _combine(ys, lpos, gate, base, ln2g, ln2b, tab, *, interpret=False):
    N = base.shape[0]
    TT = TOK_TILE
    nt = N // TT
    return pl.pallas_call(
        functools.partial(_combine_kernel, tile=TT), grid=(nt,),
        in_specs=[
            pl.BlockSpec((1, 1, 2 * N_EXPERTS), lambda i: (i, 0, 0), memory_space=pltpu.SMEM),
            pl.BlockSpec((1, TOP_K, TT), lambda i: (i, 0, 0)),
            pl.BlockSpec((1, TOP_K, TT), lambda i: (i, 0, 0)),
            pl.BlockSpec((TT, D_MODEL), lambda i: (i, 0)),
            pl.BlockSpec((1, D_MODEL), lambda i: (0, 0)),
            pl.BlockSpec((1, D_MODEL), lambda i: (0, 0)),
            pl.BlockSpec(memory_space=pl.ANY),
        ],
        out_specs=pl.BlockSpec((TT, D_MODEL), lambda i: (i, 0)),
        out_shape=jax.ShapeDtypeStruct((N, D_MODEL), F32),
        scratch_shapes=[pltpu.VMEM((_local_rows(TT), D_MODEL), MXU_DTYPE), pltpu.SemaphoreType.DMA(())],
        name="combine",
        compiler_params=pltpu.CompilerParams(
            dimension_semantics=("arbitrary",), vmem_limit_bytes=VMEM_LIMIT_BYTES),
        interpret=interpret,
    )(tab, lpos, gate, base, ln2g, ln2b, ys)


def _moe_weights(exp_w_gate, exp_w_up, exp_w_down, ln2_g, ln2_b):
    return (jnp.concatenate([exp_w_gate, exp_w_up], axis=-1).astype(MXU_DTYPE), exp_w_down.astype(MXU_DTYPE),
            ln2_g.reshape(1, D_MODEL), ln2_b.reshape(1, D_MODEL))


def _row_block(n_tokens):
    rb = ROW_BLOCK
    while rb > LANES and n_tokens * TOP_K // N_EXPERTS < rb:
        rb //= 2
    return rb


def _sorted_layout(pcnt, rb):
    nt = pcnt.shape[0]
    run_len = jnp.sum(pcnt, axis=0)
    region = (run_len + rb - 1) // rb * rb
    region_end = jnp.cumsum(region)
    region_start = region_end - region
    gbase = region_start[None, :] + jnp.cumsum(pcnt, axis=0) - pcnt
    tab = jnp.concatenate([pcnt, gbase], axis=1).reshape(nt, 1, 2 * N_EXPERTS).astype(I32)
    tail = jnp.concatenate([region_start + run_len, (region - run_len) // BF16_ROWS]
                           ).reshape(1, 1, 2 * N_EXPERTS).astype(I32)
    max_rows = nt * (TOK_TILE * TOP_K + N_EXPERTS * (BF16_ROWS - 1)) + N_EXPERTS * (rb - BF16_ROWS)
    nblk = -(-max_rows // rb)
    blk_row = jnp.arange(nblk, dtype=I32) * rb
    blk_e = jnp.minimum(jnp.sum((region_end[None, :] <= blk_row[:, None]).astype(I32), axis=1), N_EXPERTS - 1)
    nact = (region_end[-1] // rb).reshape(1).astype(I32)
    return tab, tail, blk_e, nact, nblk * rb


def _layer(x, gbuf0, s0, lbuf0, h0, mix_w, post_w, moe_w, *, chunk, reset_first, interpret=False):
    B, T, D = x.shape
    mix_in, gbuf, s_new, lbuf, h_new = _mixer(x, gbuf0, s0, lbuf0, h0, mix_w, chunk=chunk,
                                              reset_first=reset_first, interpret=interpret)
    N = B * T
    h1b, base, lpos, gate, pcnt = _post(mix_in.reshape(N, -1), x.reshape(N, D), post_w, interpret=interpret)
    rb = _row_block(N)
    tab, tail, blk_e, nact, n_rows = _sorted_layout(pcnt[:, :, 0], rb)
    wgu, wd, ln2g, ln2b = moe_w
    xs = _dispatch(h1b, lpos, tab, tail, n_rows, interpret=interpret)
    ys = _experts(xs, wgu, wd, blk_e, nact, rb, interpret=interpret)
    y = _combine(ys, lpos, gate, base, ln2g, ln2b, tab, interpret=interpret)
    keep = CONV_PAD - (CONV_W - 1)
    states = (gbuf[:, keep:], s_new, lbuf[:, keep:], h_new.reshape(B, D))
    return y.reshape(B, T, D), states


def kernel(x_prompt, x_sample, state_gdn_conv, state_gdn, state_lru_conv, state_lru, w_in, gdn_conv_w, gdn_a_log, gdn_dt_bias, gdn_norm_w, lru_conv_w, lru_conv_b, lru_wa, lru_ba, lru_wx, lru_bx, lru_lambda, w_out, ln1_g, ln1_b, router_w, router_bias, exp_w_gate, exp_w_up, exp_w_down, sh_w_gate, sh_w_up, sh_w_down, ln2_g, ln2_b):
    mix_w = _mixer_weights(w_in[0], gdn_conv_w[0], gdn_a_log[0], gdn_dt_bias[0], gdn_norm_w[0], lru_conv_w[0],
                           lru_conv_b[0], lru_wa[0], lru_ba[0], lru_wx[0], lru_bx[0], lru_lambda[0])
    post_w = _post_weights(w_out[0], ln1_g[0], ln1_b[0], router_w[0], router_bias[0],
                           sh_w_gate[0], sh_w_up[0], sh_w_down[0])
    moe_w = _moe_weights(exp_w_gate[0], exp_w_up[0], exp_w_down[0], ln2_g[0], ln2_b[0])
    bp, bs = x_prompt.shape[0], x_sample.shape[0]
    zeros = lambda *s: jnp.zeros(s, F32)
    pad_rows = lambda a: jnp.pad(a, ((0, 0), (CONV_PAD - (CONV_W - 1), 0), (0, 0)))
    yp, sp = _layer(x_prompt, zeros(bp, CONV_PAD, QKV_GROUPS * LANES), zeros(bp, HEADS, HEAD_DIM, HEAD_DIM),
                    zeros(bp, CONV_PAD, D_MODEL), zeros(bp, LRU_BLOCKS, 1, LANES),
                    mix_w, post_w, moe_w, chunk=64, reset_first=True)
    ys, ss = _layer(x_sample, pad_rows(state_gdn_conv[0]), state_gdn[0], pad_rows(state_lru_conv[0]),
                    state_lru[0].reshape(bs, LRU_BLOCKS, 1, LANES),
                    mix_w, post_w, moe_w, chunk=x_sample.shape[1], reset_first=False)
    return (yp, ys) + tuple(a[None] for a in sp) + tuple(a[None] for a in ss)
```
